```python
import math
import jax, jax.numpy as jnp
from jax import lax
import numpy as np

D_MODEL = 1024
BATCH = 8
SEQ = 2048
DEPTH = 2

HEAD_DIM = 64
SB_HEADS = 8
SWA_Q_HEADS = 8
SWA_KV_HEADS = 2
SWA_GROUP = SWA_Q_HEADS // SWA_KV_HEADS
WINDOW = 128
BLOCK = 128
N_BUCKETS = 32
MAX_DISTANCE = 128
D_FF = 2816
EPS = 1e-6
NEG_INF = -1e30

SB_WIDTH = SB_HEADS * HEAD_DIM
SWA_WIDTH = SWA_Q_HEADS * HEAD_DIM
KV_WIDTH = SWA_KV_HEADS * HEAD_DIM
MIX_WIDTH = SB_WIDTH + SWA_WIDTH
IN_WIDTH = 3 * SB_WIDTH + SWA_WIDTH + 2 * KV_WIDTH
SPLITS = (SB_WIDTH, 2 * SB_WIDTH, 3 * SB_WIDTH,
          3 * SB_WIDTH + SWA_WIDTH, 3 * SB_WIDTH + SWA_WIDTH + KV_WIDTH)

kernel_name = "hymba_stickbreak_swa_sink_macaron"


def rms_norm(x, g):
    xf = x.astype(jnp.float32)
    y = xf * lax.rsqrt(jnp.mean(xf * xf, axis=-1, keepdims=True) + EPS)
    return (y * g.astype(jnp.float32)).astype(x.dtype)


def swiglu(x, w_gu, w_down):
    gate, up = jnp.split(x @ w_gu, 2, axis=-1)
    return (jax.nn.silu(gate) * up) @ w_down


def t5_causal_bucket(dist):
    max_exact = N_BUCKETS // 2
    d = jnp.maximum(dist, 1).astype(jnp.float32)
    large = max_exact + (jnp.log(d / max_exact) / math.log(MAX_DISTANCE / max_exact)
                         * (N_BUCKETS - max_exact)).astype(jnp.int32)
    large = jnp.minimum(large, N_BUCKETS - 1)
    return jnp.where(dist < max_exact, dist, large)


def stick_breaking_attention(q, k, v):
    B, S = q.shape[:2]
    nblk = S // BLOCK
    outs = []
    for i in range(nblk):
        L = (i + 1) * BLOCK
        qi = q[:, i * BLOCK:L]
        z = jnp.einsum('bqhd,bkhd->bhqk', qi, k[:, :L],
                       preferred_element_type=jnp.float32) * (HEAD_DIM ** -0.5)
        t_pos = i * BLOCK + jnp.arange(BLOCK)[:, None]
        s_pos = jnp.arange(L)[None, :]
        causal = s_pos < t_pos
        neg_log_keep = jnp.where(causal, jax.nn.softplus(z), 0.0)
        suffix = lax.cumsum(neg_log_keep, axis=3, reverse=True) - neg_log_keep
        log_w = jax.nn.log_sigmoid(z) - suffix
        w = jnp.where(causal, jnp.exp(log_w), 0.0).astype(v.dtype)
        outs.append(jnp.einsum('bhqk,bkhd->bqhd', w, v[:, :L]))
    return jnp.concatenate(outs, axis=1).reshape(B, S, SB_WIDTH)


def sliding_window_sink_attention(q, k, v, sinks, rel_bias):
    B, S = q.shape[:2]
    nblk = S // BLOCK
    qb = q.reshape(B, nblk, BLOCK, SWA_KV_HEADS, SWA_GROUP, HEAD_DIM)

    def band(t):
        tp = jnp.pad(t, ((0, 0), (BLOCK, 0), (0, 0), (0, 0)))
        prev = tp[:, :S].reshape(B, nblk, BLOCK, SWA_KV_HEADS, HEAD_DIM)
        cur = t.reshape(B, nblk, BLOCK, SWA_KV_HEADS, HEAD_DIM)
        return jnp.concatenate([prev, cur], axis=2)

    kb, vb = band(k), band(v)
    scores = jnp.einsum('bnqhgd,bnkhd->bnhgqk', qb, kb,
                        preferred_element_type=jnp.float32) * (HEAD_DIM ** -0.5)
    a = jnp.arange(BLOCK)[:, None]
    c = jnp.arange(2 * BLOCK)[None, :]
    dist = BLOCK + a - c
    bias = rel_bias.astype(jnp.float32)[t5_causal_bucket(jnp.maximum(dist, 0))]
    bias = bias.transpose(2, 0, 1).reshape(SWA_KV_HEADS, SWA_GROUP, BLOCK, 2 * BLOCK)
    in_band = (dist >= 0) & (dist < WINDOW)
    key_exists = (jnp.arange(nblk)[:, None] > 0) | (c >= BLOCK)
    mask = in_band[None] & key_exists[:, None, :]
    scores = jnp.where(mask[None, :, None, None], scores + bias, NEG_INF)
    sink = sinks.astype(jnp.float32).reshape(SWA_KV_HEADS, SWA_GROUP)[None, None, :, :, None, None]
    m = jnp.maximum(jnp.max(scores, axis=-1, keepdims=True), sink)
    p = jnp.exp(scores - m)
    p = (p / (jnp.sum(p, axis=-1, keepdims=True) + jnp.exp(sink - m))).astype(v.dtype)
    out = jnp.einsum('bnhgqk,bnkhd->bnqhgd', p, vb)
    return out.reshape(B, S, SWA_WIDTH)


def setup_inputs(seed: int = 0) -> dict:
    key = jax.random.key(seed)
    ks = jax.random.split(key, 16)
    f32 = jnp.float32

    def w(k, shape, fan_in):
        return jax.random.normal(k, shape, f32) * (fan_in ** -0.5)

    def gain(k, shape):
        return 1.0 + 0.02 * jax.random.normal(k, shape, f32)

    return {
        "x": jax.random.normal(ks[0], (BATCH, SEQ, D_MODEL), f32),
        "norm_ffn1": gain(ks[1], (DEPTH, D_MODEL)),
        "w_ffn1_gu": w(ks[2], (DEPTH, D_MODEL, 2 * D_FF), D_MODEL),
        "w_ffn1_down": w(ks[3], (DEPTH, D_FF, D_MODEL), D_FF),
        "norm_mix": gain(ks[4], (DEPTH, D_MODEL)),
        "w_in": w(ks[5], (DEPTH, D_MODEL, IN_WIDTH), D_MODEL),
        "sinks": 0.5 * jax.random.normal(ks[6], (DEPTH, SWA_Q_HEADS), f32),
        "norm_out_sb": gain(ks[7], (DEPTH, SB_WIDTH)),
        "norm_out_swa": gain(ks[8], (DEPTH, SWA_WIDTH)),
        "w_out": w(ks[9], (DEPTH, MIX_WIDTH, D_MODEL), MIX_WIDTH),
        "norm_ffn2": gain(ks[10], (DEPTH, D_MODEL)),
        "w_ffn2_gu": w(ks[11], (DEPTH, D_MODEL, 2 * D_FF), D_MODEL),
        "w_ffn2_down": w(ks[12], (DEPTH, D_FF, D_MODEL), D_FF),
        "rel_bias": 0.5 * jax.random.normal(ks[13], (N_BUCKETS, SWA_Q_HEADS), f32),
        "norm_final": gain(ks[14], (D_MODEL,)),
    }


def reference(x, norm_ffn1, w_ffn1_gu, w_ffn1_down, norm_mix, w_in, sinks,
              norm_out_sb, norm_out_swa, w_out, norm_ffn2, w_ffn2_gu, w_ffn2_down,
              rel_bias, norm_final):
    B, S, _ = x.shape
    h = x
    for l in range(DEPTH):
        h = h + 0.5 * swiglu(rms_norm(h, norm_ffn1[l]), w_ffn1_gu[l], w_ffn1_down[l])
        n = rms_norm(h, norm_mix[l])
        proj = n @ w_in[l]
        q_sb, k_sb, v_sb, q_sw, k_sw, v_sw = jnp.split(proj, SPLITS, axis=-1)
        o_sb = stick_breaking_attention(
            q_sb.reshape(B, S, SB_HEADS, HEAD_DIM),
            k_sb.reshape(B, S, SB_HEADS, HEAD_DIM),
            v_sb.reshape(B, S, SB_HEADS, HEAD_DIM))
        o_sw = sliding_window_sink_attention(
            q_sw.reshape(B, S, SWA_Q_HEADS, HEAD_DIM),
            k_sw.reshape(B, S, SWA_KV_HEADS, HEAD_DIM),
            v_sw.reshape(B, S, SWA_KV_HEADS, HEAD_DIM),
            sinks[l], rel_bias)
        mixed = jnp.concatenate([rms_norm(o_sb, norm_out_sb[l]),
                                 rms_norm(o_sw, norm_out_swa[l])], axis=-1)
        h = h + mixed @ w_out[l]
        h = h + 0.5 * swiglu(rms_norm(h, norm_ffn2[l]), w_ffn2_gu[l], w_ffn2_down[l])
    return rms_norm(h, norm_final)
```

```python
import functools
import math

import jax
import jax.numpy as jnp
import numpy as np
from jax import lax
from jax.experimental import pallas as pl
from jax.experimental.pallas import tpu as pltpu

D_MODEL = 1024
HEAD_DIM = 64
SB_HEADS = 8
SWA_Q_HEADS = 8
SWA_KV_HEADS = 2
SWA_GROUP = SWA_Q_HEADS // SWA_KV_HEADS
WINDOW = 128
BLOCK = 128
N_BUCKETS = 32
MAX_DISTANCE = 128
D_FF = 2816
EPS = 1e-6
NEG_INF = -1e30

SB_WIDTH = SB_HEADS * HEAD_DIM
SWA_WIDTH = SWA_Q_HEADS * HEAD_DIM
KV_WIDTH = SWA_KV_HEADS * HEAD_DIM
MIX_WIDTH = SB_WIDTH + SWA_WIDTH
IN_WIDTH = 3 * SB_WIDTH + SWA_WIDTH + 2 * KV_WIDTH
Q_SCALE = HEAD_DIM ** -0.5

LANES = 128
V7X_VMEM_BYTES = 64 * 1024 * 1024
VMEM_LIMIT_BYTES = 56 * 1024 * 1024

TOKEN_TILE = 512
FF_CHUNK = 256
PROJ_CHUNK = 256
SB_TILE = 256

F32 = jnp.float32
BF16 = jnp.bfloat16


def _rms_norm(x, g):
    return x * lax.rsqrt(jnp.mean(x * x, axis=-1, keepdims=True) + EPS) * g


def _const_spec(shape):
    return pl.BlockSpec(shape, lambda *_: (0,) * len(shape), pipeline_mode=pl.Buffered(1))


def _swiglu_half_step(h, g_ref, wgu_ref, wd_ref, n_scr, a_scr):
    n_scr[...] = _rms_norm(h, g_ref[...]).astype(BF16)
    for c in range(D_FF // FF_CHUNK):
        lo = c * FF_CHUNK
        n = n_scr[...]
        gate = jnp.dot(n, wgu_ref[:, lo:lo + FF_CHUNK], preferred_element_type=F32)
        up = jnp.dot(n, wgu_ref[:, D_FF + lo:D_FF + lo + FF_CHUNK], preferred_element_type=F32)
        act = gate / (1.0 + jnp.exp(-gate)) * up
        a_scr[:, lo:lo + FF_CHUNK] = act.astype(BF16)
    return h + 0.5 * jnp.dot(a_scr[...], wd_ref[...], preferred_element_type=F32)


def _ffn_in_kernel(h_ref, g1_ref, wgu_ref, wd_ref, gm_ref, win_ref, h_out_ref, proj_ref, n_scr, a_scr):
    h1 = _swiglu_half_step(h_ref[...], g1_ref, wgu_ref, wd_ref, n_scr, a_scr)
    h_out_ref[...] = h1
    n_scr[...] = _rms_norm(h1, gm_ref[...]).astype(BF16)
    for c in range(IN_WIDTH // PROJ_CHUNK):
        lo = c * PROJ_CHUNK
        p = jnp.dot(n_scr[...], win_ref[:, lo:lo + PROJ_CHUNK], preferred_element_type=F32)
        is_query = lo < SB_WIDTH or 3 * SB_WIDTH <= lo < 3 * SB_WIDTH + SWA_WIDTH
        if is_query:
            p = p * Q_SCALE
        proj_ref[:, lo:lo + PROJ_CHUNK] = p.astype(BF16)


def _out_ffn_kernel(final_norm, osb_ref, osw_ref, h_ref, gsb_ref, gsw_ref, wout_ref, g2_ref, wgu_ref, wd_ref,
                    gf_ref, out_ref, n_scr, a_scr):
    sb = _rms_norm(osb_ref[...], gsb_ref[...]).astype(BF16)
    sw = _rms_norm(osw_ref[...], gsw_ref[...]).astype(BF16)
    h2 = (h_ref[...]
          + jnp.dot(sb, wout_ref[:SB_WIDTH, :], preferred_element_type=F32)
          + jnp.dot(sw, wout_ref[SB_WIDTH:, :], preferred_element_type=F32))
    h3 = _swiglu_half_step(h2, g2_ref, wgu_ref, wd_ref, n_scr, a_scr)
    if final_norm:
        h3 = _rms_norm(h3, gf_ref[...])
    out_ref[...] = h3


def _sb_kernel(q_ref, k_ref, v_ref, u_ref, o_ref):
    T = SB_TILE
    qi = pl.program_id(2)
    q = q_ref[...]
    lane = lax.broadcasted_iota(jnp.int32, (T, LANES), 1)
    row = lax.broadcasted_iota(jnp.int32, (T, T), 0)
    col = lax.broadcasted_iota(jnp.int32, (T, T), 1)
    causal = col < row
    qh = [jnp.where(lane < HEAD_DIM, q, jnp.zeros_like(q)), jnp.where(lane >= HEAD_DIM, q, jnp.zeros_like(q))]

    def tile(kb, carry, masked):
        start = pl.multiple_of(kb * T, T)
        k = k_ref[pl.ds(start, T), :]
        v = v_ref[pl.ds(start, T), :]
        new = []
        for hh in range(2):
            R, acc = carry[hh]
            z = lax.dot_general(qh[hh], k, (((1,), (1,)), ((), ())), preferred_element_type=F32)
            t = jnp.log(1.0 + jnp.exp(-jnp.abs(z)))
            sp = jnp.maximum(z, 0.0) + t
            ls = jnp.minimum(z, 0.0) - t
            if masked:
                sp = jnp.where(causal, sp, 0.0)
            hi = sp.astype(BF16)
            lo = (sp - hi.astype(F32)).astype(BF16)
            incl = jnp.dot(jnp.concatenate([hi, lo], axis=1), u_ref[...], preferred_element_type=F32)
            w = jnp.exp(ls - (incl - sp) - R)
            if masked:
                w = jnp.where(causal, w, 0.0)
            acc = acc + jnp.dot(w.astype(BF16), v, preferred_element_type=F32)
            R = R + incl[:, 0:1]
            new.append((R, acc))
        return tuple(new)

    zero = (jnp.zeros((T, 1), F32), jnp.zeros((T, LANES), F32))
    carry = tile(qi, (zero, zero), masked=True)
    carry = lax.fori_loop(0, qi, lambda i, c: tile(qi - 1 - i, c, masked=False), carry)
    o_ref[...] = jnp.where(lane < HEAD_DIM, carry[0][1], carry[1][1])


def _swa_kernel(sink_ref, q_ref, kp_ref, kc_ref, vp_ref, vc_ref, bias_ref, o_ref):
    qi = pl.program_id(1)
    a = lax.broadcasted_iota(jnp.int32, (BLOCK, 2 * BLOCK), 0)
    c = lax.broadcasted_iota(jnp.int32, (BLOCK, 2 * BLOCK), 1)
    dist = BLOCK + a - c
    mask = (dist >= 0) & (dist < WINDOW) & ((qi > 0) | (c >= BLOCK))
    kcat = jnp.concatenate([kp_ref[...], kc_ref[...]], axis=0)
    vcat = jnp.concatenate([vp_ref[...], vc_ref[...]], axis=0)
    outs = []
    for h in range(SWA_Q_HEADS):
        g = h // SWA_GROUP
        qh = q_ref[:, h * HEAD_DIM:(h + 1) * HEAD_DIM]
        kg = kcat[:, g * HEAD_DIM:(g + 1) * HEAD_DIM]
        vg = vcat[:, g * HEAD_DIM:(g + 1) * HEAD_DIM]
        s = lax.dot_general(qh, kg, (((1,), (1,)), ((), ())), preferred_element_type=F32)
        s = jnp.where(mask, s + bias_ref[h], NEG_INF)
        sink = sink_ref[h]
        m = jnp.maximum(jnp.max(s, axis=-1, keepdims=True), sink)
        p = jnp.exp(s - m)
        p = p / (jnp.sum(p, axis=-1, keepdims=True) + jnp.exp(sink - m))
        outs.append(jnp.dot(p.astype(BF16), vg, preferred_element_type=F32))
    o_ref[...] = jnp.concatenate(outs, axis=1)


def _t5_causal_bucket(dist):
    max_exact = N_BUCKETS // 2
    d = jnp.maximum(dist, 1).astype(F32)
    large = max_exact + (jnp.log(d / max_exact) / math.log(MAX_DISTANCE / max_exact)
                         * (N_BUCKETS - max_exact)).astype(jnp.int32)
    large = jnp.minimum(large, N_BUCKETS - 1)
    return jnp.where(dist < max_exact, dist, large)


def _swa_bias_table(rel_bias):
    a = jnp.arange(BLOCK)[:, None]
    c = jnp.arange(2 * BLOCK)[None, :]
    dist = BLOCK + a - c
    bias = rel_bias.astype(F32)[_t5_causal_bucket(jnp.maximum(dist, 0))]
    return bias.transpose(2, 0, 1)


def _params(*semantics):
    return pltpu.CompilerParams(dimension_semantics=semantics, vmem_limit_bytes=VMEM_LIMIT_BYTES)


def _ffn_in(h, g1, wgu, wd, gm, win):
    n_tok = h.shape[0]
    tm = TOKEN_TILE
    row = lambda i: (i, 0)
    return pl.pallas_call(
        _ffn_in_kernel,
        out_shape=(jax.ShapeDtypeStruct((n_tok, D_MODEL), F32), jax.ShapeDtypeStruct((n_tok, IN_WIDTH), BF16)),
        grid=(n_tok // tm,),
        in_specs=[pl.BlockSpec((tm, D_MODEL), row), _const_spec((1, D_MODEL)), _const_spec((D_MODEL, 2 * D_FF)),
                  _const_spec((D_FF, D_MODEL)), _const_spec((1, D_MODEL)), _const_spec((D_MODEL, IN_WIDTH))],
        out_specs=(pl.BlockSpec((tm, D_MODEL), row), pl.BlockSpec((tm, IN_WIDTH), row)),
        scratch_shapes=[pltpu.VMEM((tm, D_MODEL), BF16), pltpu.VMEM((tm, D_FF), BF16)],
        compiler_params=_params("parallel"),
        name="ffn_in",
    )(h, g1, wgu, wd, gm, win)


def _out_ffn(o_sb, o_sw, h, gsb, gsw, wout, g2, wgu, wd, gf, final_norm):
    n_tok = h.shape[0]
    tm = TOKEN_TILE
    row = lambda i: (i, 0)
    return pl.pallas_call(
        functools.partial(_out_ffn_kernel, final_norm),
        out_shape=jax.ShapeDtypeStruct((n_tok, D_MODEL), F32),
        grid=(n_tok // tm,),
        in_specs=[pl.BlockSpec((tm, SB_WIDTH), row), pl.BlockSpec((tm, SWA_WIDTH), row),
                  pl.BlockSpec((tm, D_MODEL), row), _const_spec((1, SB_WIDTH)), _const_spec((1, SWA_WIDTH)),
                  _const_spec((MIX_WIDTH, D_MODEL)), _const_spec((1, D_MODEL)), _const_spec((D_MODEL, 2 * D_FF)),
                  _const_spec((D_FF, D_MODEL)), _const_spec((1, D_MODEL))],
        out_specs=pl.BlockSpec((tm, D_MODEL), row),
        scratch_shapes=[pltpu.VMEM((tm, D_MODEL), BF16), pltpu.VMEM((tm, D_FF), BF16)],
        compiler_params=_params("parallel"),
        name="out_ffn",
    )(o_sb, o_sw, h, gsb, gsw, wout, g2, wgu, wd, gf)


def _sb_attention(proj, u2, batch, seq):
    T = SB_TILE
    nq = seq // T
    pairs = SB_WIDTH // LANES
    return pl.pallas_call(
        _sb_kernel,
        out_shape=jax.ShapeDtypeStruct((batch * seq, SB_WIDTH), F32),
        grid=(batch, pairs, nq),
        in_specs=[pl.BlockSpec((T, LANES), lambda b, p, i: (b * nq + i, p)),
                  pl.BlockSpec((seq, LANES), lambda b, p, i: (b, pairs + p)),
                  pl.BlockSpec((seq, LANES), lambda b, p, i: (b, 2 * pairs + p)),
                  _const_spec((2 * T, T))],
        out_specs=pl.BlockSpec((T, LANES), lambda b, p, i: (b * nq + i, p)),
        compiler_params=_params("parallel", "parallel", "arbitrary"),
        name="sb_attn",
    )(proj, proj, proj, u2)


def _swa_attention(proj, sinks, bias, batch, seq):
    nb = seq // BLOCK
    q_col = 3 * SB_WIDTH // SWA_WIDTH
    k_col = (3 * SB_WIDTH + SWA_WIDTH) // KV_WIDTH
    cur = lambda col: (lambda b, i: (b * nb + i, col))
    prev = lambda col: (lambda b, i: (b * nb + jnp.maximum(i - 1, 0), col))
    return pl.pallas_call(
        _swa_kernel,
        out_shape=jax.ShapeDtypeStruct((batch * seq, SWA_WIDTH), F32),
        grid=(batch, nb),
        in_specs=[pl.BlockSpec(memory_space=pltpu.SMEM),
                  pl.BlockSpec((BLOCK, SWA_WIDTH), cur(q_col)),
                  pl.BlockSpec((BLOCK, KV_WIDTH), prev(k_col)), pl.BlockSpec((BLOCK, KV_WIDTH), cur(k_col)),
                  pl.BlockSpec((BLOCK, KV_WIDTH), prev(k_col + 1)), pl.BlockSpec((BLOCK, KV_WIDTH), cur(k_col + 1)),
                  _const_spec((SWA_Q_HEADS, BLOCK, 2 * BLOCK))],
        out_specs=pl.BlockSpec((BLOCK, SWA_WIDTH), lambda b, i: (b * nb + i, 0)),
        compiler_params=_params("parallel", "arbitrary"),
        name="swa_attn",
    )(sinks, proj, proj, proj, proj, proj, bias)


def kernel(x, norm_ffn1, w_ffn1_gu, w_ffn1_down, norm_mix, w_in, sinks, norm_out_sb, norm_out_swa, w_out,
           norm_ffn2, w_ffn2_gu, w_ffn2_down, rel_bias, norm_final):
    batch, seq, d_model = x.shape
    depth = w_in.shape[0]
    assert d_model == D_MODEL and seq % SB_TILE == 0 and (batch * seq) % TOKEN_TILE == 0

    j = np.arange(2 * SB_TILE)[:, None] % SB_TILE
    s = np.arange(SB_TILE)[None, :]
    u2 = jnp.asarray(j >= s, dtype=BF16)
    bias = _swa_bias_table(rel_bias)
    row = lambda g: g.reshape(1, -1).astype(F32)

    h = x.reshape(batch * seq, d_model)
    for l in range(depth):
        h, proj = _ffn_in(h, row(norm_ffn1[l]), w_ffn1_gu[l].astype(BF16), w_ffn1_down[l].astype(BF16),
                          row(norm_mix[l]), w_in[l].astype(BF16))
        o_sb = _sb_attention(proj, u2, batch, seq)
        o_sw = _swa_attention(proj, sinks[l].astype(F32), bias, batch, seq)
        h = _out_ffn(o_sb, o_sw, h, row(norm_out_sb[l]), row(norm_out_swa[l]), w_out[l].astype(BF16),
                     row(norm_ffn2[l]), w_ffn2_gu[l].astype(BF16), w_ffn2_down[l].astype(BF16),
                     row(norm_final), final_norm=(l == depth - 1))
    return h.reshape(batch, seq, d_model)
```

```python
import functools
import math

import jax
import jax.numpy as jnp
import numpy as np
from jax import lax
from jax.experimental import pallas as pl
from jax.experimental.pallas import tpu as pltpu

D_MODEL = 1024
HEAD_DIM = 64
SB_HEADS = 8
SWA_Q_HEADS = 8
SWA_KV_HEADS = 2
SWA_GROUP = SWA_Q_HEADS // SWA_KV_HEADS
WINDOW = 128
BLOCK = 128
N_BUCKETS = 32
MAX_DISTANCE = 128
D_FF = 2816
EPS = 1e-6
NEG_INF = -1e30

SB_WIDTH = SB_HEADS * HEAD_DIM
SWA_WIDTH = SWA_Q_HEADS * HEAD_DIM
KV_WIDTH = SWA_KV_HEADS * HEAD_DIM
MIX_WIDTH = SB_WIDTH + SWA_WIDTH
IN_WIDTH = 3 * SB_WIDTH + SWA_WIDTH + 2 * KV_WIDTH
Q_SCALE = HEAD_DIM ** -0.5
LOG2E = math.log2(math.e)

LANES = 128
V7X_VMEM_BYTES = 64 * 1024 * 1024
VMEM_LIMIT_BYTES = 56 * 1024 * 1024

TOKEN_TILE = 512
FF_CHUNK = 256
PROJ_CHUNK = 256
SB_TILE = 256
SLOT_FIELDS = 5

F32 = jnp.float32
BF16 = jnp.bfloat16


def _rms_norm(x, g):
    return x * lax.rsqrt(jnp.mean(x * x, axis=-1, keepdims=True) + EPS) * g


def _const_spec(shape):
    return pl.BlockSpec(shape, lambda *_: (0,) * len(shape), pipeline_mode=pl.Buffered(1))


def _swiglu_half_step(h, g_ref, wgu_ref, wd_ref, n_scr, a_scr):
    n_scr[...] = _rms_norm(h, g_ref[...]).astype(BF16)
    for c in range(D_FF // FF_CHUNK):
        lo = c * FF_CHUNK
        n = n_scr[...]
        gate = jnp.dot(n, wgu_ref[:, lo:lo + FF_CHUNK], preferred_element_type=F32)
        up = jnp.dot(n, wgu_ref[:, D_FF + lo:D_FF + lo + FF_CHUNK], preferred_element_type=F32)
        act = gate / (1.0 + jnp.exp(-gate)) * up
        a_scr[:, lo:lo + FF_CHUNK] = act.astype(BF16)
    return h + 0.5 * jnp.dot(a_scr[...], wd_ref[...], preferred_element_type=F32)


def _ffn_in_kernel(h_ref, g1_ref, wgu_ref, wd_ref, gm_ref, win_ref, h_out_ref, proj_ref, n_scr, a_scr):
    h1 = _swiglu_half_step(h_ref[...], g1_ref, wgu_ref, wd_ref, n_scr, a_scr)
    h_out_ref[...] = h1
    n_scr[...] = _rms_norm(h1, gm_ref[...]).astype(BF16)
    for c in range(IN_WIDTH // PROJ_CHUNK):
        lo = c * PROJ_CHUNK
        p = jnp.dot(n_scr[...], win_ref[:, lo:lo + PROJ_CHUNK], preferred_element_type=F32)
        is_query = lo < SB_WIDTH or 3 * SB_WIDTH <= lo < 3 * SB_WIDTH + SWA_WIDTH
        if is_query:
            p = p * Q_SCALE
        proj_ref[:, lo:lo + PROJ_CHUNK] = p.astype(BF16)


def _out_ffn_kernel(final_norm, osb_ref, osw_ref, h_ref, gsb_ref, gsw_ref, wout_ref, g2_ref, wgu_ref, wd_ref,
                    gf_ref, out_ref, n_scr, a_scr):
    sb = _rms_norm(osb_ref[...], gsb_ref[...]).astype(BF16)
    sw = _rms_norm(osw_ref[...], gsw_ref[...]).astype(BF16)
    h2 = (h_ref[...]
          + jnp.dot(sb, wout_ref[:SB_WIDTH, :], preferred_element_type=F32)
          + jnp.dot(sw, wout_ref[SB_WIDTH:, :], preferred_element_type=F32))
    h3 = _swiglu_half_step(h2, g2_ref, wgu_ref, wd_ref, n_scr, a_scr)
    if final_norm:
        h3 = _rms_norm(h3, gf_ref[...])
    out_ref[...] = h3


def _sb_schedule(nq):
    nxt = list(range(nq))
    remaining = lambda a: nxt[a] + 1
    diag_slots, off_slots = [], []
    diag_left = list(range(nq - 1, -1, -1))
    while diag_left:
        a = diag_left.pop(0)
        nxt[a] -= 1
        ready = [b for b in range(nq) if b != a and 0 <= nxt[b] < b]
        if ready:
            b = max(ready, key=remaining)
            diag_slots.append((a, a, b, nxt[b], 0))
        else:
            b = diag_left.pop(0)
            diag_slots.append((a, a, b, b, 1))
        nxt[b] -= 1
    while any(n >= 0 for n in nxt):
        b1, b2 = sorted((b for b in range(nq) if nxt[b] >= 0), key=remaining, reverse=True)[:2]
        off_slots.append((b1, nxt[b1], b2, nxt[b2], 0))
        nxt[b1] -= 1
        nxt[b2] -= 1
    return diag_slots, off_slots


def _sb_kernel(n_diag_slots, n_off_slots, tbl_ref, q_ref, k_ref, v_ref, u_ref, o_ref, acc_ref, r_ref):
    T = SB_TILE
    lane = lax.broadcasted_iota(jnp.int32, (T, LANES), 1)
    row = lax.broadcasted_iota(jnp.int32, (T, T), 0)
    col = lax.broadcasted_iota(jnp.int32, (T, T), 1)
    causal = col < row
    head_lanes = [lane < HEAD_DIM, lane >= HEAD_DIM]

    acc_ref[...] = jnp.zeros_like(acc_ref)
    r_ref[...] = jnp.zeros_like(r_ref)

    def slot(jobs):
        items = []
        for qt, kt, mask in jobs:
            qs = pl.multiple_of(qt * T, T)
            ks = pl.multiple_of(kt * T, T)
            q = q_ref[pl.ds(qs, T), :]
            k = k_ref[pl.ds(ks, T), :]
            v = v_ref[pl.ds(ks, T), :]
            for hh in range(2):
                qh = jnp.where(head_lanes[hh], q, jnp.zeros_like(q))
                items.append((hh, qs, qh, k, v, mask))
        zs = []
        for hh, qs, qh, k, v, mask in items:
            z = lax.dot_general(qh, k, (((1,), (1,)), ((), ())), preferred_element_type=F32) * LOG2E
            if mask is not None:
                z = jnp.where(mask, z, NEG_INF)
            zs.append(z)
        lhs = []
        for z in zs:
            neg_abs = lax.bitcast_convert_type(
                lax.bitcast_convert_type(z, jnp.uint32) | jnp.uint32(0x80000000), F32)
            sp = jnp.maximum(z, 0.0) + jnp.log2(1.0 + jnp.exp2(neg_abs))
            hi = sp.astype(BF16)
            lo = (sp - hi.astype(F32)).astype(BF16)
            lhs.append(jnp.concatenate([hi, lo], axis=1))
        incls = [jnp.dot(x, u_ref[...], preferred_element_type=F32) for x in lhs]
        ws = []
        for (hh, qs, _, _, _, _), z, incl in zip(items, zs, incls):
            r = r_ref[hh, pl.ds(qs, T), :]
            ws.append(jnp.exp2(z - incl - jnp.concatenate([r, r], axis=1)).astype(BF16))
            r_ref[hh, pl.ds(qs, T), :] = r + jnp.broadcast_to(incl[:, 0:1], (T, LANES))
        pvs = [jnp.dot(w, item[4], preferred_element_type=F32) for w, item in zip(ws, items)]
        for (hh, qs, _, _, _, _), pv in zip(items, pvs):
            acc_ref[hh, pl.ds(qs, T), :] += pv

    def diag_slot(i, _):
        base = i * SLOT_FIELDS
        slot([(tbl_ref[base], tbl_ref[base + 1], causal),
              (tbl_ref[base + 2], tbl_ref[base + 3], causal | (tbl_ref[base + 4] == 0))])
        return 0

    def off_slot(i, _):
        base = (n_diag_slots + i) * SLOT_FIELDS
        slot([(tbl_ref[base], tbl_ref[base + 1], None), (tbl_ref[base + 2], tbl_ref[base + 3], None)])
        return 0

    lax.fori_loop(0, n_diag_slots, diag_slot, 0)
    lax.fori_loop(0, n_off_slots, off_slot, 0)
    o_ref[...] = jnp.where(lax.broadcasted_iota(jnp.int32, o_ref.shape, 1) < HEAD_DIM, acc_ref[0], acc_ref[1])


def _swa_kernel(sink_ref, q_ref, kp_ref, kc_ref, vp_ref, vc_ref, bias_ref, o_ref):
    qi = pl.program_id(1)
    a = lax.broadcasted_iota(jnp.int32, (BLOCK, 2 * BLOCK), 0)
    c = lax.broadcasted_iota(jnp.int32, (BLOCK, 2 * BLOCK), 1)
    dist = BLOCK + a - c
    mask = (dist >= 0) & (dist < WINDOW) & ((qi > 0) | (c >= BLOCK))
    kcat = jnp.concatenate([kp_ref[...], kc_ref[...]], axis=0)
    vcat = jnp.concatenate([vp_ref[...], vc_ref[...]], axis=0)
    outs = []
    for h in range(SWA_Q_HEADS):
        g = h // SWA_GROUP
        qh = q_ref[:, h * HEAD_DIM:(h + 1) * HEAD_DIM]
        kg = kcat[:, g * HEAD_DIM:(g + 1) * HEAD_DIM]
        vg = vcat[:, g * HEAD_DIM:(g + 1) * HEAD_DIM]
        s = lax.dot_general(qh, kg, (((1,), (1,)), ((), ())), preferred_element_type=F32)
        s = jnp.where(mask, s + bias_ref[h], NEG_INF)
        sink = sink_ref[h]
        m = jnp.maximum(jnp.max(s, axis=-1, keepdims=True), sink)
        p = jnp.exp(s - m)
        p = p / (jnp.sum(p, axis=-1, keepdims=True) + jnp.exp(sink - m))
        outs.append(jnp.dot(p.astype(BF16), vg, preferred_element_type=F32))
    o_ref[...] = jnp.concatenate(outs, axis=1)


def _t5_causal_bucket(dist):
    max_exact = N_BUCKETS // 2
    d = jnp.maximum(dist, 1).astype(F32)
    large = max_exact + (jnp.log(d / max_exact) / math.log(MAX_DISTANCE / max_exact)
                         * (N_BUCKETS - max_exact)).astype(jnp.int32)
    large = jnp.minimum(large, N_BUCKETS - 1)
    return jnp.where(dist < max_exact, dist, large)


def _swa_bias_table(rel_bias):
    period = 3 * BLOCK - 1
    dist = jnp.arange(period) - (BLOCK - 1)
    g = rel_bias.astype(F32)[_t5_causal_bucket(jnp.maximum(dist, 0))].T
    flat = jnp.tile(g, (1, BLOCK + 1))[:, :BLOCK * (period + 1)]
    return flat.reshape(SWA_Q_HEADS, BLOCK, period + 1)[:, :, :2 * BLOCK][:, :, ::-1]


def _params(*semantics):
    return pltpu.CompilerParams(dimension_semantics=semantics, vmem_limit_bytes=VMEM_LIMIT_BYTES)


def _ffn_in(h, g1, wgu, wd, gm, win):
    n_tok = h.shape[0]
    tm = TOKEN_TILE
    row = lambda i: (i, 0)
    return pl.pallas_call(
        _ffn_in_kernel,
        out_shape=(jax.ShapeDtypeStruct((n_tok, D_MODEL), F32), jax.ShapeDtypeStruct((n_tok, IN_WIDTH), BF16)),
        grid=(n_tok // tm,),
        in_specs=[pl.BlockSpec((tm, D_MODEL), row), _const_spec((1, D_MODEL)), _const_spec((D_MODEL, 2 * D_FF)),
                  _const_spec((D_FF, D_MODEL)), _const_spec((1, D_MODEL)), _const_spec((D_MODEL, IN_WIDTH))],
        out_specs=(pl.BlockSpec((tm, D_MODEL), row), pl.BlockSpec((tm, IN_WIDTH), row)),
        scratch_shapes=[pltpu.VMEM((tm, D_MODEL), BF16), pltpu.VMEM((tm, D_FF), BF16)],
        compiler_params=_params("parallel"),
        name="ffn_in",
    )(h, g1, wgu, wd, gm, win)


def _out_ffn(o_sb, o_sw, h, gsb, gsw, wout, g2, wgu, wd, gf, final_norm):
    n_tok = h.shape[0]
    tm = TOKEN_TILE
    row = lambda i: (i, 0)
    return pl.pallas_call(
        functools.partial(_out_ffn_kernel, final_norm),
        out_shape=jax.ShapeDtypeStruct((n_tok, D_MODEL), F32),
        grid=(n_tok // tm,),
        in_specs=[pl.BlockSpec((tm, SB_WIDTH), row), pl.BlockSpec((tm, SWA_WIDTH), row),
                  pl.BlockSpec((tm, D_MODEL), row), _const_spec((1, SB_WIDTH)), _const_spec((1, SWA_WIDTH)),
                  _const_spec((MIX_WIDTH, D_MODEL)), _const_spec((1, D_MODEL)), _const_spec((D_MODEL, 2 * D_FF)),
                  _const_spec((D_FF, D_MODEL)), _const_spec((1, D_MODEL))],
        out_specs=pl.BlockSpec((tm, D_MODEL), row),
        scratch_shapes=[pltpu.VMEM((tm, D_MODEL), BF16), pltpu.VMEM((tm, D_FF), BF16)],
        compiler_params=_params("parallel"),
        name="out_ffn",
    )(o_sb, o_sw, h, gsb, gsw, wout, g2, wgu, wd, gf)


def _sb_attention(proj, u2, batch, seq):
    T = SB_TILE
    pairs = SB_WIDTH // LANES
    diag_slots, off_slots = _sb_schedule(seq // T)
    table = jnp.asarray(np.array(diag_slots + off_slots, dtype=np.int32).reshape(-1))
    col_block = lambda c: (lambda b, p, tbl: (b, c * pairs + p))
    return pl.pallas_call(
        functools.partial(_sb_kernel, len(diag_slots), len(off_slots)),
        out_shape=jax.ShapeDtypeStruct((batch * seq, SB_WIDTH), F32),
        grid_spec=pltpu.PrefetchScalarGridSpec(
            num_scalar_prefetch=1,
            grid=(batch, pairs),
            in_specs=[pl.BlockSpec((seq, LANES), col_block(0)), pl.BlockSpec((seq, LANES), col_block(1)),
                      pl.BlockSpec((seq, LANES), col_block(2)),
                      pl.BlockSpec((2 * T, T), lambda b, p, tbl: (0, 0), pipeline_mode=pl.Buffered(1))],
            out_specs=pl.BlockSpec((seq, LANES), col_block(0)),
            scratch_shapes=[pltpu.VMEM((2, seq, LANES), F32), pltpu.VMEM((2, seq, LANES), F32)]),
        compiler_params=_params("parallel", "parallel"),
        name="sb_attn",
    )(table, proj, proj, proj, u2)


def _swa_attention(proj, sinks, bias, batch, seq):
    nb = seq // BLOCK
    q_col = 3 * SB_WIDTH // SWA_WIDTH
    k_col = (3 * SB_WIDTH + SWA_WIDTH) // KV_WIDTH
    cur = lambda col: (lambda b, i: (b * nb + i, col))
    prev = lambda col: (lambda b, i: (b * nb + jnp.maximum(i - 1, 0), col))
    return pl.pallas_call(
        _swa_kernel,
        out_shape=jax.ShapeDtypeStruct((batch * seq, SWA_WIDTH), F32),
        grid=(batch, nb),
        in_specs=[pl.BlockSpec(memory_space=pltpu.SMEM),
                  pl.BlockSpec((BLOCK, SWA_WIDTH), cur(q_col)),
                  pl.BlockSpec((BLOCK, KV_WIDTH), prev(k_col)), pl.BlockSpec((BLOCK, KV_WIDTH), cur(k_col)),
                  pl.BlockSpec((BLOCK, KV_WIDTH), prev(k_col + 1)), pl.BlockSpec((BLOCK, KV_WIDTH), cur(k_col + 1)),
                  _const_spec((SWA_Q_HEADS, BLOCK, 2 * BLOCK))],
        out_specs=pl.BlockSpec((BLOCK, SWA_WIDTH), lambda b, i: (b * nb + i, 0)),
        compiler_params=_params("parallel", "arbitrary"),
        name="swa_attn",
    )(sinks, proj, proj, proj, proj, proj, bias)


def kernel(x, norm_ffn1, w_ffn1_gu, w_ffn1_down, norm_mix, w_in, sinks, norm_out_sb, norm_out_swa, w_out,
           norm_ffn2, w_ffn2_gu, w_ffn2_down, rel_bias, norm_final):
    batch, seq, d_model = x.shape
    depth = w_in.shape[0]
    assert d_model == D_MODEL and seq % SB_TILE == 0 and (batch * seq) % TOKEN_TILE == 0

    j = np.arange(2 * SB_TILE)[:, None] % SB_TILE
    s = np.arange(SB_TILE)[None, :]
    u2 = jnp.asarray(j >= s, dtype=BF16)
    bias = _swa_bias_table(rel_bias)
    row = lambda g: g.reshape(1, -1).astype(F32)

    h = x.reshape(batch * seq, d_model)
    for l in range(depth):
        h, proj = _ffn_in(h, row(norm_ffn1[l]), w_ffn1_gu[l].astype(BF16), w_ffn1_down[l].astype(BF16),
                          row(norm_mix[l]), w_in[l].astype(BF16))
        o_sb = _sb_attention(proj, u2, batch, seq)
        o_sw = _swa_attention(proj, sinks[l].astype(F32), bias, batch, seq)
        h = _out_ffn(o_sb, o_sw, h, row(norm_out_sb[l]), row(norm_out_swa[l]), w_out[l].astype(BF16),
                     row(norm_ffn2[l]), w_ffn2_gu[l].astype(BF16), w_ffn2_down[l].astype(BF16),
                     row(norm_final), final_norm=(l == depth - 1))
    return h.reshape(batch, seq, d_model)
```

```python
import functools
import math

import jax
import jax.numpy as jnp
import numpy as np
from jax import lax
from jax.experimental import pallas as pl
from jax.experimental.pallas import tpu as pltpu

D_MODEL = 1024
HEAD_DIM = 64
SB_HEADS = 8
SWA_Q_HEADS = 8
SWA_KV_HEADS = 2
SWA_GROUP = SWA_Q_HEADS // SWA_KV_HEADS
WINDOW = 128
BLOCK = 128
N_BUCKETS = 32
MAX_DISTANCE = 128
D_FF = 2816
EPS = 1e-6
NEG_INF = -1e30

SB_WIDTH = SB_HEADS * HEAD_DIM
SWA_WIDTH = SWA_Q_HEADS * HEAD_DIM
KV_WIDTH = SWA_KV_HEADS * HEAD_DIM
MIX_WIDTH = SB_WIDTH + SWA_WIDTH
IN_WIDTH = 3 * SB_WIDTH + SWA_WIDTH + 2 * KV_WIDTH
Q_SCALE = HEAD_DIM ** -0.5
LOG2E = math.log2(math.e)

LANES = 128
V7X_VMEM_BYTES = 64 * 1024 * 1024
VMEM_LIMIT_BYTES = 56 * 1024 * 1024

TOKEN_TILE = 512
FF_CHUNK = 256
PROJ_CHUNK = 256
SB_TILE = 256
SLOT_FIELDS = 5

F32 = jnp.float32
BF16 = jnp.bfloat16


def _rms_norm(x, g):
    return x * lax.rsqrt(jnp.mean(x * x, axis=-1, keepdims=True) + EPS) * g


def _const_spec(shape):
    return pl.BlockSpec(shape, lambda *_: (0,) * len(shape), pipeline_mode=pl.Buffered(1))


def _swiglu_half_step(h, g_ref, wgu_ref, wd_ref, n_scr, a_scr):
    n_scr[...] = _rms_norm(h, g_ref[...]).astype(BF16)
    for c in range(D_FF // FF_CHUNK):
        lo = c * FF_CHUNK
        n = n_scr[...]
        gate = jnp.dot(n, wgu_ref[:, lo:lo + FF_CHUNK], preferred_element_type=F32)
        up = jnp.dot(n, wgu_ref[:, D_FF + lo:D_FF + lo + FF_CHUNK], preferred_element_type=F32)
        act = gate / (1.0 + jnp.exp(-gate)) * up
        a_scr[:, lo:lo + FF_CHUNK] = act.astype(BF16)
    return h + 0.5 * jnp.dot(a_scr[...], wd_ref[...], preferred_element_type=F32)


def _ffn_in_kernel(h_ref, g1_ref, wgu_ref, wd_ref, gm_ref, win_ref, h_out_ref, proj_ref, n_scr, a_scr):
    h1 = _swiglu_half_step(h_ref[...], g1_ref, wgu_ref, wd_ref, n_scr, a_scr)
    h_out_ref[...] = h1
    n_scr[...] = _rms_norm(h1, gm_ref[...]).astype(BF16)
    for c in range(IN_WIDTH // PROJ_CHUNK):
        lo = c * PROJ_CHUNK
        p = jnp.dot(n_scr[...], win_ref[:, lo:lo + PROJ_CHUNK], preferred_element_type=F32)
        is_query = lo < SB_WIDTH or 3 * SB_WIDTH <= lo < 3 * SB_WIDTH + SWA_WIDTH
        if is_query:
            p = p * Q_SCALE
        proj_ref[:, lo:lo + PROJ_CHUNK] = p.astype(BF16)


def _out_ffn_kernel(final_norm, osb_ref, osw_ref, h_ref, gsb_ref, gsw_ref, wout_ref, g2_ref, wgu_ref, wd_ref,
                    gf_ref, out_ref, n_scr, a_scr):
    sb = _rms_norm(osb_ref[...], gsb_ref[...]).astype(BF16)
    sw = _rms_norm(osw_ref[...], gsw_ref[...]).astype(BF16)
    h2 = (h_ref[...]
          + jnp.dot(sb, wout_ref[:SB_WIDTH, :], preferred_element_type=F32)
          + jnp.dot(sw, wout_ref[SB_WIDTH:, :], preferred_element_type=F32))
    h3 = _swiglu_half_step(h2, g2_ref, wgu_ref, wd_ref, n_scr, a_scr)
    if final_norm:
        h3 = _rms_norm(h3, gf_ref[...])
    out_ref[...] = h3


def _sb_schedule(nq):
    nxt = list(range(nq))
    remaining = lambda a: nxt[a] + 1
    diag_slots, off_slots = [], []
    diag_left = list(range(nq - 1, -1, -1))
    while diag_left:
        a = diag_left.pop(0)
        nxt[a] -= 1
        ready = [b for b in range(nq) if b != a and 0 <= nxt[b] < b]
        if ready:
            b = max(ready, key=remaining)
            diag_slots.append((a, a, b, nxt[b], 0))
        else:
            b = diag_left.pop(0)
            diag_slots.append((a, a, b, b, 1))
        nxt[b] -= 1
    while any(n >= 0 for n in nxt):
        b1, b2 = sorted((b for b in range(nq) if nxt[b] >= 0), key=remaining, reverse=True)[:2]
        off_slots.append((b1, nxt[b1], b2, nxt[b2], 0))
        nxt[b1] -= 1
        nxt[b2] -= 1
    return diag_slots, off_slots


def _sb_kernel(n_diag_slots, n_off_slots, tbl_ref, q_ref, k_ref, v_ref, u_ref, o_ref, acc_ref, r_ref):
    T = SB_TILE
    lane = lax.broadcasted_iota(jnp.int32, (T, LANES), 1)
    row = lax.broadcasted_iota(jnp.int32, (T, T), 0)
    col = lax.broadcasted_iota(jnp.int32, (T, T), 1)
    causal = col < row
    head_lanes = [lane < HEAD_DIM, lane >= HEAD_DIM]

    acc_ref[...] = jnp.zeros_like(acc_ref)
    r_ref[...] = jnp.zeros_like(r_ref)

    class Slot:
        def __init__(self, index, masked):
            base = index * SLOT_FIELDS
            masks = [causal, causal | (tbl_ref[base + 4] == 0)] if masked else [None, None]
            self.items = []
            for n in range(2):
                qs = pl.multiple_of(tbl_ref[base + 2 * n] * T, T)
                ks = pl.multiple_of(tbl_ref[base + 2 * n + 1] * T, T)
                self.items += [(hh, qs, ks, masks[n]) for hh in range(2)]

        def scores(self):
            self.z = []
            for hh, qs, ks, mask in self.items:
                q = q_ref[pl.ds(qs, T), :]
                qh = jnp.where(head_lanes[hh], q, jnp.zeros_like(q))
                z = lax.dot_general(qh, k_ref[pl.ds(ks, T), :], (((1,), (1,)), ((), ())),
                                    preferred_element_type=F32) * LOG2E
                if mask is not None:
                    z = jnp.where(mask, z, NEG_INF)
                self.z.append(z)

        def softplus(self):
            self.lhs = []
            for z in self.z:
                neg_abs = lax.bitcast_convert_type(
                    lax.bitcast_convert_type(z, jnp.uint32) | jnp.uint32(0x80000000), F32)
                sp = jnp.maximum(z, 0.0) + jnp.log2(1.0 + jnp.exp2(neg_abs))
                self.lhs.append(sp.astype(BF16))

        def suffix_sums(self):
            self.incl = [jnp.dot(x, u_ref[...], preferred_element_type=F32) for x in self.lhs]

        def weights(self):
            self.w = []
            for (hh, qs, _, _), z, incl in zip(self.items, self.z, self.incl):
                r = r_ref[hh, pl.ds(qs, T), :]
                self.w.append(jnp.exp2(z - incl - jnp.concatenate([r, r], axis=1)).astype(BF16))
                r_ref[hh, pl.ds(qs, T), :] = r + jnp.broadcast_to(incl[:, 0:1], (T, LANES))

        def values(self):
            self.pv = [jnp.dot(w, v_ref[pl.ds(ks, T), :], preferred_element_type=F32)
                       for (_, _, ks, _), w in zip(self.items, self.w)]

        def accumulate(self):
            for (hh, qs, _, _), pv in zip(self.items, self.pv):
                acc_ref[hh, pl.ds(qs, T), :] += pv

    def slot_pair(a, b):
        a.scores()
        a.softplus()
        a.suffix_sums()
        b.scores()
        a.weights()
        a.values()
        b.softplus()
        a.accumulate()
        b.suffix_sums()
        b.weights()
        b.values()
        b.accumulate()

    def pair_loop(first, n_pairs, masked):
        def body(i, _):
            slot_pair(Slot(first + 2 * i, masked), Slot(first + 2 * i + 1, masked))
            return 0
        lax.fori_loop(0, n_pairs, body, 0)

    assert n_diag_slots % 2 == 1 and n_off_slots % 2 == 1
    pair_loop(0, n_diag_slots // 2, True)
    slot_pair(Slot(n_diag_slots - 1, True), Slot(n_diag_slots, False))
    pair_loop(n_diag_slots + 1, n_off_slots // 2, False)
    o_ref[...] = jnp.where(lax.broadcasted_iota(jnp.int32, o_ref.shape, 1) < HEAD_DIM, acc_ref[0], acc_ref[1])


def _swa_kernel(sink_ref, q_ref, kp_ref, kc_ref, vp_ref, vc_ref, bias_ref, o_ref):
    qi = pl.program_id(1)
    a = lax.broadcasted_iota(jnp.int32, (BLOCK, 2 * BLOCK), 0)
    c = lax.broadcasted_iota(jnp.int32, (BLOCK, 2 * BLOCK), 1)
    dist = BLOCK + a - c
    mask = (dist >= 0) & (dist < WINDOW) & ((qi > 0) | (c >= BLOCK))
    kcat = jnp.concatenate([kp_ref[...], kc_ref[...]], axis=0)
    vcat = jnp.concatenate([vp_ref[...], vc_ref[...]], axis=0)
    outs = []
    for h in range(SWA_Q_HEADS):
        g = h // SWA_GROUP
        qh = q_ref[:, h * HEAD_DIM:(h + 1) * HEAD_DIM]
        kg = kcat[:, g * HEAD_DIM:(g + 1) * HEAD_DIM]
        vg = vcat[:, g * HEAD_DIM:(g + 1) * HEAD_DIM]
        s = lax.dot_general(qh, kg, (((1,), (1,)), ((), ())), preferred_element_type=F32)
        s = jnp.where(mask, s + bias_ref[h], NEG_INF)
        sink = sink_ref[h]
        m = jnp.maximum(jnp.max(s, axis=-1, keepdims=True), sink)
        p = jnp.exp(s - m)
        p = p / (jnp.sum(p, axis=-1, keepdims=True) + jnp.exp(sink - m))
        outs.append(jnp.dot(p.astype(BF16), vg, preferred_element_type=F32))
    o_ref[...] = jnp.concatenate(outs, axis=1)


def _t5_causal_bucket(dist):
    max_exact = N_BUCKETS // 2
    d = jnp.maximum(dist, 1).astype(F32)
    large = max_exact + (jnp.log(d / max_exact) / math.log(MAX_DISTANCE / max_exact)
                         * (N_BUCKETS - max_exact)).astype(jnp.int32)
    large = jnp.minimum(large, N_BUCKETS - 1)
    return jnp.where(dist < max_exact, dist, large)


def _swa_bias_table(rel_bias):
    period = 3 * BLOCK - 1
    dist = jnp.arange(period) - (BLOCK - 1)
    g = rel_bias.astype(F32)[_t5_causal_bucket(jnp.maximum(dist, 0))].T
    flat = jnp.tile(g, (1, BLOCK + 1))[:, :BLOCK * (period + 1)]
    return flat.reshape(SWA_Q_HEADS, BLOCK, period + 1)[:, :, :2 * BLOCK][:, :, ::-1]


def _params(*semantics):
    return pltpu.CompilerParams(dimension_semantics=semantics, vmem_limit_bytes=VMEM_LIMIT_BYTES)


def _ffn_in(h, g1, wgu, wd, gm, win):
    n_tok = h.shape[0]
    tm = TOKEN_TILE
    row = lambda i: (i, 0)
    return pl.pallas_call(
        _ffn_in_kernel,
        out_shape=(jax.ShapeDtypeStruct((n_tok, D_MODEL), F32), jax.ShapeDtypeStruct((n_tok, IN_WIDTH), BF16)),
        grid=(n_tok // tm,),
        in_specs=[pl.BlockSpec((tm, D_MODEL), row), _const_spec((1, D_MODEL)), _const_spec((D_MODEL, 2 * D_FF)),
                  _const_spec((D_FF, D_MODEL)), _const_spec((1, D_MODEL)), _const_spec((D_MODEL, IN_WIDTH))],
        out_specs=(pl.BlockSpec((tm, D_MODEL), row), pl.BlockSpec((tm, IN_WIDTH), row)),
        scratch_shapes=[pltpu.VMEM((tm, D_MODEL), BF16), pltpu.VMEM((tm, D_FF), BF16)],
        compiler_params=_params("parallel"),
        name="ffn_in",
    )(h, g1, wgu, wd, gm, win)


def _out_ffn(o_sb, o_sw, h, gsb, gsw, wout, g2, wgu, wd, gf, final_norm):
    n_tok = h.shape[0]
    tm = TOKEN_TILE
    row = lambda i: (i, 0)
    return pl.pallas_call(
        functools.partial(_out_ffn_kernel, final_norm),
        out_shape=jax.ShapeDtypeStruct((n_tok, D_MODEL), F32),
        grid=(n_tok // tm,),
        in_specs=[pl.BlockSpec((tm, SB_WIDTH), row), pl.BlockSpec((tm, SWA_WIDTH), row),
                  pl.BlockSpec((tm, D_MODEL), row), _const_spec((1, SB_WIDTH)), _const_spec((1, SWA_WIDTH)),
                  _const_spec((MIX_WIDTH, D_MODEL)), _const_spec((1, D_MODEL)), _const_spec((D_MODEL, 2 * D_FF)),
                  _const_spec((D_FF, D_MODEL)), _const_spec((1, D_MODEL))],
        out_specs=pl.BlockSpec((tm, D_MODEL), row),
        scratch_shapes=[pltpu.VMEM((tm, D_MODEL), BF16), pltpu.VMEM((tm, D_FF), BF16)],
        compiler_params=_params("parallel"),
        name="out_ffn",
    )(o_sb, o_sw, h, gsb, gsw, wout, g2, wgu, wd, gf)


def _sb_attention(proj, u2, batch, seq):
    T = SB_TILE
    pairs = SB_WIDTH // LANES
    diag_slots, off_slots = _sb_schedule(seq // T)
    table = jnp.asarray(np.array(diag_slots + off_slots, dtype=np.int32).reshape(-1))
    col_block = lambda c: (lambda b, p, tbl: (b, c * pairs + p))
    return pl.pallas_call(
        functools.partial(_sb_kernel, len(diag_slots), len(off_slots)),
        out_shape=jax.ShapeDtypeStruct((batch * seq, SB_WIDTH), F32),
        grid_spec=pltpu.PrefetchScalarGridSpec(
            num_scalar_prefetch=1,
            grid=(batch, pairs),
            in_specs=[pl.BlockSpec((seq, LANES), col_block(0)), pl.BlockSpec((seq, LANES), col_block(1)),
                      pl.BlockSpec((seq, LANES), col_block(2)),
                      pl.BlockSpec((T, T), lambda b, p, tbl: (0, 0), pipeline_mode=pl.Buffered(1))],
            out_specs=pl.BlockSpec((seq, LANES), col_block(0)),
            scratch_shapes=[pltpu.VMEM((2, seq, LANES), F32), pltpu.VMEM((2, seq, LANES), F32)]),
        compiler_params=_params("parallel", "parallel"),
        name="sb_attn",
    )(table, proj, proj, proj, u2)


def _swa_attention(proj, sinks, bias, batch, seq):
    nb = seq // BLOCK
    q_col = 3 * SB_WIDTH // SWA_WIDTH
    k_col = (3 * SB_WIDTH + SWA_WIDTH) // KV_WIDTH
    cur = lambda col: (lambda b, i: (b * nb + i, col))
    prev = lambda col: (lambda b, i: (b * nb + jnp.maximum(i - 1, 0), col))
    return pl.pallas_call(
        _swa_kernel,
        out_shape=jax.ShapeDtypeStruct((batch * seq, SWA_WIDTH), F32),
        grid=(batch, nb),
        in_specs=[pl.BlockSpec(memory_space=pltpu.SMEM),
                  pl.BlockSpec((BLOCK, SWA_WIDTH), cur(q_col)),
                  pl.BlockSpec((BLOCK, KV_WIDTH), prev(k_col)), pl.BlockSpec((BLOCK, KV_WIDTH), cur(k_col)),
                  pl.BlockSpec((BLOCK, KV_WIDTH), prev(k_col + 1)), pl.BlockSpec((BLOCK, KV_WIDTH), cur(k_col + 1)),
                  _const_spec((SWA_Q_HEADS, BLOCK, 2 * BLOCK))],
        out_specs=pl.BlockSpec((BLOCK, SWA_WIDTH), lambda b, i: (b * nb + i, 0)),
        compiler_params=_params("parallel", "arbitrary"),
        name="swa_attn",
    )(sinks, proj, proj, proj, proj, proj, bias)


def kernel(x, norm_ffn1, w_ffn1_gu, w_ffn1_down, norm_mix, w_in, sinks, norm_out_sb, norm_out_swa, w_out,
           norm_ffn2, w_ffn2_gu, w_ffn2_down, rel_bias, norm_final):
    batch, seq, d_model = x.shape
    depth = w_in.shape[0]
    assert d_model == D_MODEL and seq % SB_TILE == 0 and (batch * seq) % TOKEN_TILE == 0

    j = np.arange(SB_TILE)[:, None]
    s = np.arange(SB_TILE)[None, :]
    u2 = jnp.asarray(j >= s, dtype=BF16)
    bias = _swa_bias_table(rel_bias)
    row = lambda g: g.reshape(1, -1).astype(F32)

    h = x.reshape(batch * seq, d_model)
    for l in range(depth):
        h, proj = _ffn_in(h, row(norm_ffn1[l]), w_ffn1_gu[l].astype(BF16), w_ffn1_down[l].astype(BF16),
                          row(norm_mix[l]), w_in[l].astype(BF16))
        o_sb = _sb_attention(proj, u2, batch, seq)
        o_sw = _swa_attention(proj, sinks[l].astype(F32), bias, batch, seq)
        h = _out_ffn(o_sb, o_sw, h, row(norm_out_sb[l]), row(norm_out_swa[l]), w_out[l].astype(BF16),
                     row(norm_ffn2[l]), w_ffn2_gu[l].astype(BF16), w_ffn2_down[l].astype(BF16),
                     row(norm_final), final_norm=(l == depth - 1))
    return h.reshape(batch, seq, d_model)
```

```python
import functools
import math

import jax
import jax.numpy as jnp
import numpy as np
from jax import lax
from jax.experimental import pallas as pl
from jax.experimental.pallas import tpu as pltpu

D_MODEL = 1024
HEAD_DIM = 64
SB_HEADS = 8
SWA_Q_HEADS = 8
SWA_KV_HEADS = 2
SWA_GROUP = SWA_Q_HEADS // SWA_KV_HEADS
WINDOW = 128
BLOCK = 128
N_BUCKETS = 32
MAX_DISTANCE = 128
D_FF = 2816
EPS = 1e-6
NEG_INF = -1e30

SB_WIDTH = SB_HEADS * HEAD_DIM
SWA_WIDTH = SWA_Q_HEADS * HEAD_DIM
KV_WIDTH = SWA_KV_HEADS * HEAD_DIM
MIX_WIDTH = SB_WIDTH + SWA_WIDTH
IN_WIDTH = 3 * SB_WIDTH + SWA_WIDTH + 2 * KV_WIDTH
Q_SCALE = HEAD_DIM ** -0.5
LOG2E = math.log2(math.e)

LANES = 128
V7X_VMEM_BYTES = 64 * 1024 * 1024
VMEM_LIMIT_BYTES = 56 * 1024 * 1024

TOKEN_TILE = 512
FF_CHUNK = 256
PROJ_CHUNK = 256
SB_TILE = 256
SLOT_FIELDS = 5

F32 = jnp.float32
BF16 = jnp.bfloat16


def _rms_norm(x, g):
    return x * lax.rsqrt(jnp.mean(x * x, axis=-1, keepdims=True) + EPS) * g


def _const_spec(shape):
    return pl.BlockSpec(shape, lambda *_: (0,) * len(shape), pipeline_mode=pl.Buffered(1))


def _layer_spec(shape, layer):
    return pl.BlockSpec((None,) + shape, lambda *_: (layer,) + (0,) * len(shape), pipeline_mode=pl.Buffered(1))


def _swiglu_half_step(h, g_ref, wgu_ref, wd_ref, n_scr, a_scr):
    n_scr[...] = _rms_norm(h, g_ref[...]).astype(BF16)
    for c in range(D_FF // FF_CHUNK):
        lo = c * FF_CHUNK
        n = n_scr[...]
        gate = jnp.dot(n, wgu_ref[:, lo:lo + FF_CHUNK], preferred_element_type=F32)
        up = jnp.dot(n, wgu_ref[:, D_FF + lo:D_FF + lo + FF_CHUNK], preferred_element_type=F32)
        act = gate / (1.0 + jnp.exp(-gate)) * up
        a_scr[:, lo:lo + FF_CHUNK] = act.astype(BF16)
    return h + 0.5 * jnp.dot(a_scr[...], wd_ref[...], preferred_element_type=F32)


def _ffn_in_kernel(h_ref, g1_ref, wgu_ref, wd_ref, gm_ref, win_ref, h_out_ref, proj_ref, n_scr, a_scr):
    h1 = _swiglu_half_step(h_ref[...], g1_ref, wgu_ref, wd_ref, n_scr, a_scr)
    h_out_ref[...] = h1
    n_scr[...] = _rms_norm(h1, gm_ref[...]).astype(BF16)
    for c in range(IN_WIDTH // PROJ_CHUNK):
        lo = c * PROJ_CHUNK
        p = jnp.dot(n_scr[...], win_ref[:, lo:lo + PROJ_CHUNK], preferred_element_type=F32)
        is_query = lo < SB_WIDTH or 3 * SB_WIDTH <= lo < 3 * SB_WIDTH + SWA_WIDTH
        if is_query:
            p = p * Q_SCALE
        proj_ref[:, lo:lo + PROJ_CHUNK] = p.astype(BF16)


def _out_ffn_kernel(final_norm, osb_ref, osw_ref, h_ref, gsb_ref, gsw_ref, wout_ref, g2_ref, wgu_ref, wd_ref,
                    gf_ref, out_ref, n_scr, a_scr):
    sb = _rms_norm(osb_ref[...], gsb_ref[...]).astype(BF16)
    sw = _rms_norm(osw_ref[...], gsw_ref[...]).astype(BF16)
    h2 = (h_ref[...]
          + jnp.dot(sb, wout_ref[:SB_WIDTH, :], preferred_element_type=F32)
          + jnp.dot(sw, wout_ref[SB_WIDTH:, :], preferred_element_type=F32))
    h3 = _swiglu_half_step(h2, g2_ref, wgu_ref, wd_ref, n_scr, a_scr)
    if final_norm:
        h3 = _rms_norm(h3, gf_ref[...])
    out_ref[...] = h3


def _sb_schedule(nq):
    nxt = list(range(nq))
    remaining = lambda a: nxt[a] + 1
    diag_slots, off_slots = [], []
    diag_left = list(range(nq - 1, -1, -1))
    while diag_left:
        a = diag_left.pop(0)
        nxt[a] -= 1
        ready = [b for b in range(nq) if b != a and 0 <= nxt[b] < b]
        if ready:
            b = max(ready, key=remaining)
            diag_slots.append((a, a, b, nxt[b], 0))
        else:
            b = diag_left.pop(0)
            diag_slots.append((a, a, b, b, 1))
        nxt[b] -= 1
    while any(n >= 0 for n in nxt):
        b1, b2 = sorted((b for b in range(nq) if nxt[b] >= 0), key=remaining, reverse=True)[:2]
        off_slots.append((b1, nxt[b1], b2, nxt[b2], 0))
        nxt[b1] -= 1
        nxt[b2] -= 1
    return diag_slots, off_slots


def _sb_kernel(n_diag_slots, n_off_slots, tbl_ref, q_ref, k_ref, v_ref, u_ref, o_ref, acc_ref, r_ref):
    T = SB_TILE
    lane = lax.broadcasted_iota(jnp.int32, (T, LANES), 1)
    row = lax.broadcasted_iota(jnp.int32, (T, T), 0)
    col = lax.broadcasted_iota(jnp.int32, (T, T), 1)
    causal = col < row
    head_lanes = [lane < HEAD_DIM, lane >= HEAD_DIM]

    acc_ref[...] = jnp.zeros_like(acc_ref)
    r_ref[...] = jnp.zeros_like(r_ref)

    class Slot:
        def __init__(self, index, masked):
            base = index * SLOT_FIELDS
            masks = [causal, causal | (tbl_ref[base + 4] == 0)] if masked else [None, None]
            self.items = []
            for n in range(2):
                qs = pl.multiple_of(tbl_ref[base + 2 * n] * T, T)
                ks = pl.multiple_of(tbl_ref[base + 2 * n + 1] * T, T)
                self.items += [(hh, qs, ks, masks[n]) for hh in range(2)]

        def scores(self):
            self.z = []
            for hh, qs, ks, mask in self.items:
                q = q_ref[pl.ds(qs, T), :]
                qh = jnp.where(head_lanes[hh], q, jnp.zeros_like(q))
                z = lax.dot_general(qh, k_ref[pl.ds(ks, T), :], (((1,), (1,)), ((), ())),
                                    preferred_element_type=F32) * LOG2E
                if mask is not None:
                    z = jnp.where(mask, z, NEG_INF)
                self.z.append(z)

        def softplus(self):
            self.lhs = []
            for z in self.z:
                neg_abs = lax.bitcast_convert_type(
                    lax.bitcast_convert_type(z, jnp.uint32) | jnp.uint32(0x80000000), F32)
                sp = jnp.maximum(z, 0.0) + jnp.log2(1.0 + jnp.exp2(neg_abs))
                self.lhs.append(sp.astype(BF16))

        def suffix_sums(self):
            self.incl = [jnp.dot(x, u_ref[...], preferred_element_type=F32) for x in self.lhs]

        def weights(self):
            self.w = []
            for (hh, qs, _, _), z, incl in zip(self.items, self.z, self.incl):
                r = r_ref[hh, pl.ds(qs, T), :]
                self.w.append(jnp.exp2(z - incl - jnp.concatenate([r, r], axis=1)).astype(BF16))
                r_ref[hh, pl.ds(qs, T), :] = r + jnp.broadcast_to(incl[:, 0:1], (T, LANES))

        def values(self):
            self.pv = [jnp.dot(w, v_ref[pl.ds(ks, T), :], preferred_element_type=F32)
                       for (_, _, ks, _), w in zip(self.items, self.w)]

        def accumulate(self):
            for (hh, qs, _, _), pv in zip(self.items, self.pv):
                acc_ref[hh, pl.ds(qs, T), :] += pv

    def slot_pair(a, b):
        a.scores()
        a.softplus()
        a.suffix_sums()
        b.scores()
        a.weights()
        a.values()
        b.softplus()
        a.accumulate()
        b.suffix_sums()
        b.weights()
        b.values()
        b.accumulate()

    def pair_loop(first, n_pairs, masked):
        def body(i, _):
            slot_pair(Slot(first + 2 * i, masked), Slot(first + 2 * i + 1, masked))
            return 0
        lax.fori_loop(0, n_pairs, body, 0)

    assert n_diag_slots % 2 == 1 and n_off_slots % 2 == 1
    pair_loop(0, n_diag_slots // 2, True)
    slot_pair(Slot(n_diag_slots - 1, True), Slot(n_diag_slots, False))
    pair_loop(n_diag_slots + 1, n_off_slots // 2, False)
    o_ref[...] = jnp.where(lax.broadcasted_iota(jnp.int32, o_ref.shape, 1) < HEAD_DIM, acc_ref[0], acc_ref[1])


def _swa_kernel(layer, sink_ref, q_ref, k_ref, v_ref, bias_ref, o_ref):
    n_blocks = q_ref.shape[0] // BLOCK
    n_slots = SWA_Q_HEADS // 2
    lane = lax.broadcasted_iota(jnp.int32, (BLOCK, LANES), 1)
    a = lax.broadcasted_iota(jnp.int32, (BLOCK, BLOCK), 0)
    c = lax.broadcasted_iota(jnp.int32, (BLOCK, BLOCK), 1)
    upper = c > a
    half = [lane < HEAD_DIM, lane >= HEAD_DIM]

    heads = [(p, sub) for p in range(n_slots) for sub in range(2)]

    class Block:
        def __init__(self, i):
            self.cur = pl.multiple_of(i * BLOCK, BLOCK)
            self.prev = pl.multiple_of(jnp.maximum(i - 1, 0) * BLOCK, BLOCK)
            self.table = (i == 0).astype(jnp.int32)

        def rows(self, ref):
            return jnp.concatenate([ref[pl.ds(self.prev, BLOCK), :], ref[pl.ds(self.cur, BLOCK), :]], axis=0)

        def scores(self):
            kcat = self.rows(k_ref)
            self.s = []
            for p, sub in heads:
                q = q_ref[pl.ds(self.cur, BLOCK), p * LANES:(p + 1) * LANES]
                s = lax.dot_general(jnp.where(half[sub], q, jnp.zeros_like(q)), kcat, (((1,), (1,)), ((), ())),
                                    preferred_element_type=F32)
                self.s.append(jnp.where(upper, s[:, :BLOCK], s[:, BLOCK:]) + bias_ref[self.table, 2 * p + sub])

        def softmax(self):
            self.p = []
            for (p, sub), s in zip(heads, self.s):
                sink = sink_ref[layer, 2 * p + sub]
                m = jnp.maximum(jnp.max(s, axis=-1, keepdims=True), sink)
                e = jnp.exp(s - m)
                e = (e * (1.0 / (jnp.sum(e, axis=-1, keepdims=True) + jnp.exp(sink - m)))).astype(BF16)
                zero = jnp.zeros_like(e)
                self.p.append(jnp.concatenate([jnp.where(upper, e, zero), jnp.where(upper, zero, e)], axis=1))

        def values(self):
            vcat = self.rows(v_ref)
            self.o = [jnp.dot(pr, vcat, preferred_element_type=F32) for pr in self.p]

        def store(self):
            for p in range(n_slots):
                o_ref[pl.ds(self.cur, BLOCK), p * LANES:(p + 1) * LANES] = jnp.where(
                    half[0], self.o[2 * p], self.o[2 * p + 1])

    def block_pair(i, _):
        a, b = Block(2 * i), Block(2 * i + 1)
        a.scores()
        a.softmax()
        b.scores()
        a.values()
        b.softmax()
        a.store()
        b.values()
        b.store()
        return 0

    assert n_blocks % 2 == 0
    lax.fori_loop(0, n_blocks // 2, block_pair, 0)


def _t5_causal_bucket(dist):
    max_exact = N_BUCKETS // 2
    d = jnp.maximum(dist, 1).astype(F32)
    large = max_exact + (jnp.log(d / max_exact) / math.log(MAX_DISTANCE / max_exact)
                         * (N_BUCKETS - max_exact)).astype(jnp.int32)
    large = jnp.minimum(large, N_BUCKETS - 1)
    return jnp.where(dist < max_exact, dist, large)


def _swa_bias_table(rel_bias):
    period = 3 * BLOCK - 1
    dist = jnp.arange(period) - (BLOCK - 1)
    g = rel_bias.astype(F32)[_t5_causal_bucket(jnp.maximum(dist, 0))].T
    flat = jnp.tile(g, (1, BLOCK + 1))[:, :BLOCK * (period + 1)]
    return flat.reshape(SWA_Q_HEADS, BLOCK, period + 1)[:, :, :2 * BLOCK][:, :, ::-1]


def _params(*semantics):
    return pltpu.CompilerParams(dimension_semantics=semantics, vmem_limit_bytes=VMEM_LIMIT_BYTES)


def _ffn_in(layer, h, g1, wgu, wd, gm, win):
    n_tok = h.shape[0]
    tm = TOKEN_TILE
    row = lambda i: (i, 0)
    return pl.pallas_call(
        _ffn_in_kernel,
        out_shape=(jax.ShapeDtypeStruct((n_tok, D_MODEL), F32), jax.ShapeDtypeStruct((n_tok, IN_WIDTH), BF16)),
        grid=(n_tok // tm,),
        in_specs=[pl.BlockSpec((tm, D_MODEL), row), _layer_spec((1, D_MODEL), layer),
                  _layer_spec((D_MODEL, 2 * D_FF), layer), _layer_spec((D_FF, D_MODEL), layer),
                  _layer_spec((1, D_MODEL), layer), _layer_spec((D_MODEL, IN_WIDTH), layer)],
        out_specs=(pl.BlockSpec((tm, D_MODEL), row), pl.BlockSpec((tm, IN_WIDTH), row)),
        scratch_shapes=[pltpu.VMEM((tm, D_MODEL), BF16), pltpu.VMEM((tm, D_FF), BF16)],
        compiler_params=_params("parallel"),
        name="ffn_in",
    )(h, g1, wgu, wd, gm, win)


def _out_ffn(layer, o_sb, o_sw, h, gsb, gsw, wout, g2, wgu, wd, gf, final_norm):
    n_tok = h.shape[0]
    tm = TOKEN_TILE
    row = lambda i: (i, 0)
    return pl.pallas_call(
        functools.partial(_out_ffn_kernel, final_norm),
        out_shape=jax.ShapeDtypeStruct((n_tok, D_MODEL), F32),
        grid=(n_tok // tm,),
        in_specs=[pl.BlockSpec((tm, SB_WIDTH), row), pl.BlockSpec((tm, SWA_WIDTH), row),
                  pl.BlockSpec((tm, D_MODEL), row), _layer_spec((1, SB_WIDTH), layer),
                  _layer_spec((1, SWA_WIDTH), layer), _layer_spec((MIX_WIDTH, D_MODEL), layer),
                  _layer_spec((1, D_MODEL), layer), _layer_spec((D_MODEL, 2 * D_FF), layer),
                  _layer_spec((D_FF, D_MODEL), layer), _const_spec((1, D_MODEL))],
        out_specs=pl.BlockSpec((tm, D_MODEL), row),
        scratch_shapes=[pltpu.VMEM((tm, D_MODEL), BF16), pltpu.VMEM((tm, D_FF), BF16)],
        compiler_params=_params("parallel"),
        name="out_ffn",
    )(o_sb, o_sw, h, gsb, gsw, wout, g2, wgu, wd, gf)


def _sb_attention(proj, u2, batch, seq):
    T = SB_TILE
    pairs = SB_WIDTH // LANES
    diag_slots, off_slots = _sb_schedule(seq // T)
    table = jnp.asarray(np.array(diag_slots + off_slots, dtype=np.int32).reshape(-1))
    col_block = lambda c: (lambda b, p, tbl: (b, c * pairs + p))
    return pl.pallas_call(
        functools.partial(_sb_kernel, len(diag_slots), len(off_slots)),
        out_shape=jax.ShapeDtypeStruct((batch * seq, SB_WIDTH), F32),
        grid_spec=pltpu.PrefetchScalarGridSpec(
            num_scalar_prefetch=1,
            grid=(batch, pairs),
            in_specs=[pl.BlockSpec((seq, LANES), col_block(0)), pl.BlockSpec((seq, LANES), col_block(1)),
                      pl.BlockSpec((seq, LANES), col_block(2)),
                      pl.BlockSpec((T, T), lambda b, p, tbl: (0, 0), pipeline_mode=pl.Buffered(1))],
            out_specs=pl.BlockSpec((seq, LANES), col_block(0)),
            scratch_shapes=[pltpu.VMEM((2, seq, LANES), F32), pltpu.VMEM((2, seq, LANES), F32)]),
        compiler_params=_params("parallel", "parallel"),
        name="sb_attn",
    )(table, proj, proj, proj, u2)


def _swa_attention(layer, proj, sinks, bias, batch, seq):
    q_col = 3 * SB_WIDTH // SWA_WIDTH
    k_col = (3 * SB_WIDTH + SWA_WIDTH) // KV_WIDTH
    col_block = lambda col: (lambda b: (b, col))
    return pl.pallas_call(
        functools.partial(_swa_kernel, layer),
        out_shape=jax.ShapeDtypeStruct((batch * seq, SWA_WIDTH), F32),
        grid=(batch,),
        in_specs=[pl.BlockSpec(memory_space=pltpu.SMEM),
                  pl.BlockSpec((seq, SWA_WIDTH), col_block(q_col)),
                  pl.BlockSpec((seq, KV_WIDTH), col_block(k_col)), pl.BlockSpec((seq, KV_WIDTH), col_block(k_col + 1)),
                  _const_spec((2, SWA_Q_HEADS, BLOCK, BLOCK))],
        out_specs=pl.BlockSpec((seq, SWA_WIDTH), col_block(0)),
        compiler_params=_params("parallel"),
        name="swa_attn",
    )(sinks, proj, proj, proj, bias)


def _pair_swa_heads(x, axis):
    shape = x.shape
    k = shape[axis] // SWA_Q_HEADS
    x = x.reshape(shape[:axis] + (SWA_KV_HEADS, SWA_GROUP, k) + shape[axis + 1:])
    return jnp.swapaxes(x, axis, axis + 1).reshape(shape)


def _swa_folded_bias(rel_bias):
    table = _swa_bias_table(rel_bias)
    a = np.arange(BLOCK)[:, None]
    c = np.arange(BLOCK)[None, :]
    upper = jnp.asarray(c > a)
    normal = jnp.where(upper, table[:, :, :BLOCK], table[:, :, BLOCK:])
    first = jnp.where(upper, NEG_INF, table[:, :, BLOCK:])
    return _pair_swa_heads(jnp.stack([normal, first]), 1)


def kernel(x, norm_ffn1, w_ffn1_gu, w_ffn1_down, norm_mix, w_in, sinks, norm_out_sb, norm_out_swa, w_out,
           norm_ffn2, w_ffn2_gu, w_ffn2_down, rel_bias, norm_final):
    batch, seq, d_model = x.shape
    depth = w_in.shape[0]
    assert d_model == D_MODEL and seq % SB_TILE == 0 and (batch * seq) % TOKEN_TILE == 0

    j = np.arange(SB_TILE)[:, None]
    s = np.arange(SB_TILE)[None, :]
    u2 = jnp.asarray(j >= s, dtype=BF16)
    bias = _swa_folded_bias(rel_bias)
    q_lo, q_hi = 3 * SB_WIDTH, 3 * SB_WIDTH + SWA_WIDTH

    gain = lambda g: g.astype(F32).reshape(depth, 1, -1)
    win = jnp.concatenate([w_in[:, :, :q_lo], _pair_swa_heads(w_in[:, :, q_lo:q_hi], 2), w_in[:, :, q_hi:]],
                          axis=2).astype(BF16)
    wout = jnp.concatenate([w_out[:, :SB_WIDTH], _pair_swa_heads(w_out[:, SB_WIDTH:], 1)], axis=1).astype(BF16)
    wgu1, wd1 = w_ffn1_gu.astype(BF16), w_ffn1_down.astype(BF16)
    wgu2, wd2 = w_ffn2_gu.astype(BF16), w_ffn2_down.astype(BF16)
    g1, gm, g2, gsb = gain(norm_ffn1), gain(norm_mix), gain(norm_ffn2), gain(norm_out_sb)
    gsw = gain(_pair_swa_heads(norm_out_swa, 1))
    gf = norm_final.astype(F32).reshape(1, -1)
    sinks_paired = _pair_swa_heads(sinks.astype(F32), 1)

    h = x.reshape(batch * seq, d_model)
    for l in range(depth):
        h, proj = _ffn_in(l, h, g1, wgu1, wd1, gm, win)
        o_sb = _sb_attention(proj, u2, batch, seq)
        o_sw = _swa_attention(l, proj, sinks_paired, bias, batch, seq)
        h = _out_ffn(l, o_sb, o_sw, h, gsb, gsw, wout, g2, wgu2, wd2, gf, final_norm=(l == depth - 1))
    return h.reshape(batch, seq, d_model)
```

```python
import functools
import math

import jax
import jax.numpy as jnp
import numpy as np
from jax import lax
from jax.experimental import pallas as pl
from jax.experimental.pallas import tpu as pltpu

D_MODEL = 1024
HEAD_DIM = 64
SB_HEADS = 8
SWA_Q_HEADS = 8
SWA_KV_HEADS = 2
SWA_GROUP = SWA_Q_HEADS // SWA_KV_HEADS
WINDOW = 128
BLOCK = 128
N_BUCKETS = 32
MAX_DISTANCE = 128
D_FF = 2816
EPS = 1e-6
NEG_INF = -1e30

SB_WIDTH = SB_HEADS * HEAD_DIM
SWA_WIDTH = SWA_Q_HEADS * HEAD_DIM
KV_WIDTH = SWA_KV_HEADS * HEAD_DIM
MIX_WIDTH = SB_WIDTH + SWA_WIDTH
IN_WIDTH = 3 * SB_WIDTH + SWA_WIDTH + 2 * KV_WIDTH
Q_SCALE = HEAD_DIM ** -0.5
LOG2E = math.log2(math.e)

LANES = 128
V7X_VMEM_BYTES = 64 * 1024 * 1024
VMEM_LIMIT_BYTES = 56 * 1024 * 1024

TOKEN_TILE = 512
FF_CHUNK = 256
PROJ_CHUNK = 256
SB_TILE = 256
SLOT_FIELDS = 5

F32 = jnp.float32
BF16 = jnp.bfloat16


def _rms_norm(x, g):
    return x * lax.rsqrt(jnp.mean(x * x, axis=-1, keepdims=True) + EPS) * g


def _const_spec(shape):
    return pl.BlockSpec(shape, lambda *_: (0,) * len(shape), pipeline_mode=pl.Buffered(1))


def _layer_spec(shape, layer):
    return pl.BlockSpec((None,) + shape, lambda *_: (layer,) + (0,) * len(shape), pipeline_mode=pl.Buffered(1))


def _norm_rows(n_scr, rows, x, g_ref):
    n_scr[rows, :] = _rms_norm(x, g_ref[...]).astype(BF16)


def _gate_up_rows(rows, wgu_ref, n_scr, a_scr):
    for c in range(D_FF // FF_CHUNK):
        lo = c * FF_CHUNK
        n = n_scr[rows, :]
        gate = jnp.dot(n, wgu_ref[:, lo:lo + FF_CHUNK], preferred_element_type=F32)
        up = jnp.dot(n, wgu_ref[:, D_FF + lo:D_FF + lo + FF_CHUNK], preferred_element_type=F32)
        act = gate / (1.0 + jnp.exp(-gate)) * up
        a_scr[rows, lo:lo + FF_CHUNK] = act.astype(BF16)


def _down_rows(rows, h, wd_ref, a_scr):
    return h + 0.5 * jnp.dot(a_scr[rows, :], wd_ref[...], preferred_element_type=F32)


def _project_rows(rows, win_ref, n_scr, proj_ref):
    for c in range(IN_WIDTH // PROJ_CHUNK):
        lo = c * PROJ_CHUNK
        p = jnp.dot(n_scr[rows, :], win_ref[:, lo:lo + PROJ_CHUNK], preferred_element_type=F32)
        if lo < SB_WIDTH:
            p = p * (Q_SCALE * LOG2E)
        elif 3 * SB_WIDTH <= lo < 3 * SB_WIDTH + SWA_WIDTH:
            p = p * Q_SCALE
        proj_ref[rows, lo:lo + PROJ_CHUNK] = p.astype(BF16)


HALF_A = slice(0, TOKEN_TILE)
HALF_B = slice(TOKEN_TILE, 2 * TOKEN_TILE)


def _ffn_in_kernel(h_ref, g1_ref, wgu_ref, wd_ref, gm_ref, win_ref, h_out_ref, proj_ref, n_scr, a_scr):
    _norm_rows(n_scr, HALF_A, h_ref[HALF_A, :], g1_ref)
    _gate_up_rows(HALF_A, wgu_ref, n_scr, a_scr)
    _norm_rows(n_scr, HALF_B, h_ref[HALF_B, :], g1_ref)
    h1a = _down_rows(HALF_A, h_ref[HALF_A, :], wd_ref, a_scr)
    h_out_ref[HALF_A, :] = h1a
    _gate_up_rows(HALF_B, wgu_ref, n_scr, a_scr)
    _norm_rows(n_scr, HALF_A, h_out_ref[HALF_A, :], gm_ref)
    h1b = _down_rows(HALF_B, h_ref[HALF_B, :], wd_ref, a_scr)
    h_out_ref[HALF_B, :] = h1b
    _project_rows(HALF_A, win_ref, n_scr, proj_ref)
    _norm_rows(n_scr, HALF_B, h_out_ref[HALF_B, :], gm_ref)
    _project_rows(HALF_B, win_ref, n_scr, proj_ref)


def _out_ffn_kernel(final_norm, osb_ref, osw_ref, h_ref, gsb_ref, gsw_ref, wout_ref, g2_ref, wgu_ref, wd_ref,
                    gf_ref, out_ref, n_scr, a_scr):
    def mix(rows):
        sb = _rms_norm(osb_ref[rows, :], gsb_ref[...]).astype(BF16)
        sw = _rms_norm(osw_ref[rows, :], gsw_ref[...]).astype(BF16)
        out_ref[rows, :] = (h_ref[rows, :]
                            + jnp.dot(sb, wout_ref[:SB_WIDTH, :], preferred_element_type=F32)
                            + jnp.dot(sw, wout_ref[SB_WIDTH:, :], preferred_element_type=F32))
        _norm_rows(n_scr, rows, out_ref[rows, :], g2_ref)

    def finish(rows):
        h3 = _down_rows(rows, out_ref[rows, :], wd_ref, a_scr)
        out_ref[rows, :] = _rms_norm(h3, gf_ref[...]) if final_norm else h3

    mix(HALF_A)
    _gate_up_rows(HALF_A, wgu_ref, n_scr, a_scr)
    mix(HALF_B)
    finish(HALF_A)
    _gate_up_rows(HALF_B, wgu_ref, n_scr, a_scr)
    finish(HALF_B)


def _sb_schedule(nq):
    nxt = list(range(nq))
    remaining = lambda a: nxt[a] + 1
    diag_slots, off_slots = [], []
    diag_left = list(range(nq - 1, -1, -1))
    while diag_left:
        a = diag_left.pop(0)
        nxt[a] -= 1
        ready = [b for b in range(nq) if b != a and 0 <= nxt[b] < b]
        if ready:
            b = max(ready, key=remaining)
            diag_slots.append((a, a, b, nxt[b], 0))
        else:
            b = diag_left.pop(0)
            diag_slots.append((a, a, b, b, 1))
        nxt[b] -= 1
    while any(n >= 0 for n in nxt):
        b1, b2 = sorted((b for b in range(nq) if nxt[b] >= 0), key=remaining, reverse=True)[:2]
        off_slots.append((b1, nxt[b1], b2, nxt[b2], 0))
        nxt[b1] -= 1
        nxt[b2] -= 1
    return diag_slots, off_slots


def _sb_kernel(n_diag_slots, n_off_slots, tbl_ref, q_ref, k_ref, v_ref, u_ref, o_ref, acc_ref, r_ref):
    T = SB_TILE
    lane = lax.broadcasted_iota(jnp.int32, (T, LANES), 1)
    row = lax.broadcasted_iota(jnp.int32, (T, T), 0)
    col = lax.broadcasted_iota(jnp.int32, (T, T), 1)
    causal = col < row
    head_lanes = [lane < HEAD_DIM, lane >= HEAD_DIM]

    acc_ref[...] = jnp.zeros_like(acc_ref)
    r_ref[...] = jnp.zeros_like(r_ref)

    class Slot:
        def __init__(self, index, masked):
            base = index * SLOT_FIELDS
            masks = [causal, causal | (tbl_ref[base + 4] == 0)] if masked else [None, None]
            self.items = []
            for n in range(2):
                qs = pl.multiple_of(tbl_ref[base + 2 * n] * T, T)
                ks = pl.multiple_of(tbl_ref[base + 2 * n + 1] * T, T)
                self.items += [(hh, qs, ks, masks[n]) for hh in range(2)]

        def scores(self):
            self.z = []
            for hh, qs, ks, mask in self.items:
                q = q_ref[pl.ds(qs, T), :]
                qh = jnp.where(head_lanes[hh], q, jnp.zeros_like(q))
                z = lax.dot_general(qh, k_ref[pl.ds(ks, T), :], (((1,), (1,)), ((), ())),
                                    preferred_element_type=F32)
                if mask is not None:
                    z = jnp.where(mask, z, NEG_INF)
                self.z.append(z)

        def softplus(self):
            self.lhs = []
            for z in self.z:
                neg_abs = lax.bitcast_convert_type(
                    lax.bitcast_convert_type(z, jnp.uint32) | jnp.uint32(0x80000000), F32)
                sp = jnp.maximum(z, 0.0) + jnp.log2(1.0 + jnp.exp2(neg_abs))
                self.lhs.append(sp.astype(BF16))

        def suffix_sums(self):
            self.incl = [jnp.dot(x, u_ref[...], preferred_element_type=F32) for x in self.lhs]

        def weights(self):
            self.w = []
            for (hh, qs, _, _), z, incl in zip(self.items, self.z, self.incl):
                r = r_ref[hh, pl.ds(qs, T), :]
                self.w.append(jnp.exp2(z - incl - jnp.concatenate([r, r], axis=1)).astype(BF16))
                r_ref[hh, pl.ds(qs, T), :] = r + jnp.broadcast_to(incl[:, 0:1], (T, LANES))

        def values(self):
            self.pv = [jnp.dot(w, v_ref[pl.ds(ks, T), :], preferred_element_type=F32)
                       for (_, _, ks, _), w in zip(self.items, self.w)]

        def accumulate(self):
            for (hh, qs, _, _), pv in zip(self.items, self.pv):
                acc_ref[hh, pl.ds(qs, T), :] += pv

    def slot_pair(a, b):
        a.scores()
        a.softplus()
        a.suffix_sums()
        b.scores()
        a.weights()
        a.values()
        b.softplus()
        a.accumulate()
        b.suffix_sums()
        b.weights()
        b.values()
        b.accumulate()

    def pair_loop(first, n_pairs, masked):
        def body(i, _):
            slot_pair(Slot(first + 2 * i, masked), Slot(first + 2 * i + 1, masked))
            return 0
        lax.fori_loop(0, n_pairs, body, 0)

    assert n_diag_slots % 2 == 1 and n_off_slots % 2 == 1
    pair_loop(0, n_diag_slots // 2, True)
    slot_pair(Slot(n_diag_slots - 1, True), Slot(n_diag_slots, False))
    pair_loop(n_diag_slots + 1, n_off_slots // 2, False)
    o_ref[...] = jnp.where(lax.broadcasted_iota(jnp.int32, o_ref.shape, 1) < HEAD_DIM, acc_ref[0], acc_ref[1])


def _swa_kernel(layer, sink_ref, q_ref, k_ref, v_ref, bias_ref, o_ref):
    n_blocks = q_ref.shape[0] // BLOCK
    n_slots = SWA_Q_HEADS // 2
    lane = lax.broadcasted_iota(jnp.int32, (BLOCK, LANES), 1)
    a = lax.broadcasted_iota(jnp.int32, (BLOCK, BLOCK), 0)
    c = lax.broadcasted_iota(jnp.int32, (BLOCK, BLOCK), 1)
    upper = c > a
    half = [lane < HEAD_DIM, lane >= HEAD_DIM]

    heads = [(p, sub) for p in range(n_slots) for sub in range(2)]

    class Block:
        def __init__(self, i):
            self.cur = pl.multiple_of(i * BLOCK, BLOCK)
            self.prev = pl.multiple_of(jnp.maximum(i - 1, 0) * BLOCK, BLOCK)
            self.table = 1 - jnp.minimum(i, 1)

        def rows(self, ref):
            return jnp.concatenate([ref[pl.ds(self.prev, BLOCK), :], ref[pl.ds(self.cur, BLOCK), :]], axis=0)

        def scores(self):
            kcat = self.rows(k_ref)
            self.s = []
            for p, sub in heads:
                q = q_ref[pl.ds(self.cur, BLOCK), p * LANES:(p + 1) * LANES]
                s = lax.dot_general(jnp.where(half[sub], q, jnp.zeros_like(q)), kcat, (((1,), (1,)), ((), ())),
                                    preferred_element_type=F32)
                self.s.append(jnp.where(upper, s[:, :BLOCK], s[:, BLOCK:]) + bias_ref[self.table, 2 * p + sub])

        def softmax(self):
            self.p = []
            for (p, sub), s in zip(heads, self.s):
                sink = sink_ref[layer, 2 * p + sub]
                m = jnp.maximum(jnp.max(s, axis=-1, keepdims=True), sink)
                e = jnp.exp(s - m)
                e = (e * (1.0 / (jnp.sum(e, axis=-1, keepdims=True) + jnp.exp(sink - m)))).astype(BF16)
                zero = jnp.zeros_like(e)
                self.p.append(jnp.concatenate([jnp.where(upper, e, zero), jnp.where(upper, zero, e)], axis=1))

        def values(self):
            vcat = self.rows(v_ref)
            self.o = [jnp.dot(pr, vcat, preferred_element_type=F32) for pr in self.p]

        def store(self):
            for p in range(n_slots):
                o_ref[pl.ds(self.cur, BLOCK), p * LANES:(p + 1) * LANES] = jnp.where(
                    half[0], self.o[2 * p], self.o[2 * p + 1])

    def block_pair(i, _):
        a, b = Block(2 * i), Block(2 * i + 1)
        a.scores()
        a.softmax()
        b.scores()
        a.values()
        b.softmax()
        a.store()
        b.values()
        b.store()
        return 0

    assert n_blocks % 2 == 0
    lax.fori_loop(0, n_blocks // 2, block_pair, 0)


def _t5_causal_bucket(dist):
    max_exact = N_BUCKETS // 2
    d = jnp.maximum(dist, 1).astype(F32)
    large = max_exact + (jnp.log(d / max_exact) / math.log(MAX_DISTANCE / max_exact)
                         * (N_BUCKETS - max_exact)).astype(jnp.int32)
    large = jnp.minimum(large, N_BUCKETS - 1)
    return jnp.where(dist < max_exact, dist, large)


def _swa_bias_table(rel_bias):
    period = 3 * BLOCK - 1
    dist = jnp.arange(period) - (BLOCK - 1)
    g = rel_bias.astype(F32)[_t5_causal_bucket(jnp.maximum(dist, 0))].T
    flat = jnp.tile(g, (1, BLOCK + 1))[:, :BLOCK * (period + 1)]
    return flat.reshape(SWA_Q_HEADS, BLOCK, period + 1)[:, :, :2 * BLOCK][:, :, ::-1]


def _params(*semantics):
    return pltpu.CompilerParams(dimension_semantics=semantics, vmem_limit_bytes=VMEM_LIMIT_BYTES)


def _ffn_in(layer, h, g1, wgu, wd, gm, win):
    n_tok = h.shape[0]
    tm = 2 * TOKEN_TILE
    row = lambda i: (i, 0)
    return pl.pallas_call(
        _ffn_in_kernel,
        out_shape=(jax.ShapeDtypeStruct((n_tok, D_MODEL), F32), jax.ShapeDtypeStruct((n_tok, IN_WIDTH), BF16)),
        grid=(n_tok // tm,),
        in_specs=[pl.BlockSpec((tm, D_MODEL), row), _layer_spec((1, D_MODEL), layer),
                  _layer_spec((D_MODEL, 2 * D_FF), layer), _layer_spec((D_FF, D_MODEL), layer),
                  _layer_spec((1, D_MODEL), layer), _layer_spec((D_MODEL, IN_WIDTH), layer)],
        out_specs=(pl.BlockSpec((tm, D_MODEL), row, pipeline_mode=pl.Buffered(1)),
                   pl.BlockSpec((tm, IN_WIDTH), row, pipeline_mode=pl.Buffered(1))),
        scratch_shapes=[pltpu.VMEM((tm, D_MODEL), BF16), pltpu.VMEM((tm, D_FF), BF16)],
        compiler_params=_params("parallel"),
        name="ffn_in",
    )(h, g1, wgu, wd, gm, win)


def _out_ffn(layer, o_sb, o_sw, h, gsb, gsw, wout, g2, wgu, wd, gf, final_norm):
    n_tok = h.shape[0]
    tm = 2 * TOKEN_TILE
    row = lambda i: (i, 0)
    return pl.pallas_call(
        functools.partial(_out_ffn_kernel, final_norm),
        out_shape=jax.ShapeDtypeStruct((n_tok, D_MODEL), F32),
        grid=(n_tok // tm,),
        in_specs=[pl.BlockSpec((tm, SB_WIDTH), row), pl.BlockSpec((tm, SWA_WIDTH), row),
                  pl.BlockSpec((tm, D_MODEL), row), _layer_spec((1, SB_WIDTH), layer),
                  _layer_spec((1, SWA_WIDTH), layer), _layer_spec((MIX_WIDTH, D_MODEL), layer),
                  _layer_spec((1, D_MODEL), layer), _layer_spec((D_MODEL, 2 * D_FF), layer),
                  _layer_spec((D_FF, D_MODEL), layer), _const_spec((1, D_MODEL))],
        out_specs=pl.BlockSpec((tm, D_MODEL), row, pipeline_mode=pl.Buffered(1)),
        scratch_shapes=[pltpu.VMEM((tm, D_MODEL), BF16), pltpu.VMEM((tm, D_FF), BF16)],
        compiler_params=_params("parallel"),
        name="out_ffn",
    )(o_sb, o_sw, h, gsb, gsw, wout, g2, wgu, wd, gf)


def _sb_attention(proj, u2, batch, seq):
    T = SB_TILE
    pairs = SB_WIDTH // LANES
    diag_slots, off_slots = _sb_schedule(seq // T)
    table = jnp.asarray(np.array(diag_slots + off_slots, dtype=np.int32).reshape(-1))
    col_block = lambda c: (lambda b, p, tbl: (b, c * pairs + p))
    return pl.pallas_call(
        functools.partial(_sb_kernel, len(diag_slots), len(off_slots)),
        out_shape=jax.ShapeDtypeStruct((batch * seq, SB_WIDTH), F32),
        grid_spec=pltpu.PrefetchScalarGridSpec(
            num_scalar_prefetch=1,
            grid=(batch, pairs),
            in_specs=[pl.BlockSpec((seq, LANES), col_block(0)), pl.BlockSpec((seq, LANES), col_block(1)),
                      pl.BlockSpec((seq, LANES), col_block(2)),
                      pl.BlockSpec((T, T), lambda b, p, tbl: (0, 0), pipeline_mode=pl.Buffered(1))],
            out_specs=pl.BlockSpec((seq, LANES), col_block(0)),
            scratch_shapes=[pltpu.VMEM((2, seq, LANES), F32), pltpu.VMEM((2, seq, LANES), F32)]),
        compiler_params=_params("parallel", "parallel"),
        name="sb_attn",
    )(table, proj, proj, proj, u2)


def _swa_attention(layer, proj, sinks, bias, batch, seq):
    q_col = 3 * SB_WIDTH // SWA_WIDTH
    k_col = (3 * SB_WIDTH + SWA_WIDTH) // KV_WIDTH
    col_block = lambda col: (lambda b: (b, col))
    return pl.pallas_call(
        functools.partial(_swa_kernel, layer),
        out_shape=jax.ShapeDtypeStruct((batch * seq, SWA_WIDTH), F32),
        grid=(batch,),
        in_specs=[pl.BlockSpec(memory_space=pltpu.SMEM),
                  pl.BlockSpec((seq, SWA_WIDTH), col_block(q_col)),
                  pl.BlockSpec((seq, KV_WIDTH), col_block(k_col)), pl.BlockSpec((seq, KV_WIDTH), col_block(k_col + 1)),
                  _const_spec((2, SWA_Q_HEADS, BLOCK, BLOCK))],
        out_specs=pl.BlockSpec((seq, SWA_WIDTH), col_block(0)),
        compiler_params=_params("parallel"),
        name="swa_attn",
    )(sinks, proj, proj, proj, bias)


def _pair_swa_heads(x, axis):
    shape = x.shape
    k = shape[axis] // SWA_Q_HEADS
    x = x.reshape(shape[:axis] + (SWA_KV_HEADS, SWA_GROUP, k) + shape[axis + 1:])
    return jnp.swapaxes(x, axis, axis + 1).reshape(shape)


def _swa_folded_bias(rel_bias):
    table = _swa_bias_table(rel_bias)
    a = np.arange(BLOCK)[:, None]
    c = np.arange(BLOCK)[None, :]
    upper = jnp.asarray(c > a)
    normal = jnp.where(upper, table[:, :, :BLOCK], table[:, :, BLOCK:])
    first = jnp.where(upper, NEG_INF, table[:, :, BLOCK:])
    return _pair_swa_heads(jnp.stack([normal, first]), 1)


def kernel(x, norm_ffn1, w_ffn1_gu, w_ffn1_down, norm_mix, w_in, sinks, norm_out_sb, norm_out_swa, w_out,
           norm_ffn2, w_ffn2_gu, w_ffn2_down, rel_bias, norm_final):
    batch, seq, d_model = x.shape
    depth = w_in.shape[0]
    assert d_model == D_MODEL and seq % SB_TILE == 0 and (batch * seq) % (2 * TOKEN_TILE) == 0

    j = np.arange(SB_TILE)[:, None]
    s = np.arange(SB_TILE)[None, :]
    u2 = jnp.asarray(j >= s, dtype=BF16)
    bias = _swa_folded_bias(rel_bias)
    q_lo, q_hi = 3 * SB_WIDTH, 3 * SB_WIDTH + SWA_WIDTH

    gain = lambda g: g.astype(F32).reshape(depth, 1, -1)
    win = jnp.concatenate([w_in[:, :, :q_lo], _pair_swa_heads(w_in[:, :, q_lo:q_hi], 2), w_in[:, :, q_hi:]],
                          axis=2).astype(BF16)
    wout = jnp.concatenate([w_out[:, :SB_WIDTH], _pair_swa_heads(w_out[:, SB_WIDTH:], 1)], axis=1).astype(BF16)
    wgu1, wd1 = w_ffn1_gu.astype(BF16), w_ffn1_down.astype(BF16)
    wgu2, wd2 = w_ffn2_gu.astype(BF16), w_ffn2_down.astype(BF16)
    g1, gm, g2, gsb = gain(norm_ffn1), gain(norm_mix), gain(norm_ffn2), gain(norm_out_sb)
    gsw = gain(_pair_swa_heads(norm_out_swa, 1))
    gf = norm_final.astype(F32).reshape(1, -1)
    sinks_paired = _pair_swa_heads(sinks.astype(F32), 1)

    h = x.reshape(batch * seq, d_model)
    for l in range(depth):
        h, proj = _ffn_in(l, h, g1, wgu1, wd1, gm, win)
        o_sb = _sb_attention(proj, u2, batch, seq)
        o_sw = _swa_attention(l, proj, sinks_paired, bias, batch, seq)
        h = _out_ffn(l, o_sb, o_sw, h, gsb, gsw, wout, g2, wgu2, wd2, gf, final_norm=(l == depth - 1))
    return h.reshape(batch, seq, d_model)
```

```python
import functools
import math

import jax
import jax.numpy as jnp
import numpy as np
from jax import lax
from jax.experimental import pallas as pl
from jax.experimental.pallas import tpu as pltpu

D_MODEL = 1024
HEAD_DIM = 64
SB_HEADS = 8
SWA_Q_HEADS = 8
SWA_KV_HEADS = 2
SWA_GROUP = SWA_Q_HEADS // SWA_KV_HEADS
WINDOW = 128
BLOCK = 128
N_BUCKETS = 32
MAX_DISTANCE = 128
D_FF = 2816
EPS = 1e-6
NEG_INF = -1e30

SB_WIDTH = SB_HEADS * HEAD_DIM
SWA_WIDTH = SWA_Q_HEADS * HEAD_DIM
KV_WIDTH = SWA_KV_HEADS * HEAD_DIM
MIX_WIDTH = SB_WIDTH + SWA_WIDTH
IN_WIDTH = 3 * SB_WIDTH + SWA_WIDTH + 2 * KV_WIDTH
Q_SCALE = HEAD_DIM ** -0.5
LOG2E = math.log2(math.e)

LANES = 128
V7X_VMEM_BYTES = 64 * 1024 * 1024
VMEM_LIMIT_BYTES = 56 * 1024 * 1024

TOKEN_TILE = 256
FF_CHUNK = 256
PROJ_CHUNK = 256
SB_TILE = 256
SLOT_FIELDS = 5

F32 = jnp.float32
BF16 = jnp.bfloat16


def _rms_norm(x, g):
    return x * lax.rsqrt(jnp.mean(x * x, axis=-1, keepdims=True) + EPS) * g


def _const_spec(shape):
    return pl.BlockSpec(shape, lambda *_: (0,) * len(shape), pipeline_mode=pl.Buffered(1))


def _layer_spec(shape, layer):
    return pl.BlockSpec((None,) + shape, lambda *_: (layer,) + (0,) * len(shape), pipeline_mode=pl.Buffered(1))


def _norm_rows(n_scr, rows, x, g_ref):
    n_scr[rows, :] = _rms_norm(x, g_ref[...]).astype(BF16)


def _gate_up_rows(rows, wgu_ref, n_scr, a_scr):
    for c in range(D_FF // FF_CHUNK):
        lo = c * FF_CHUNK
        n = n_scr[rows, :]
        gate = jnp.dot(n, wgu_ref[:, lo:lo + FF_CHUNK], preferred_element_type=F32)
        up = jnp.dot(n, wgu_ref[:, D_FF + lo:D_FF + lo + FF_CHUNK], preferred_element_type=F32)
        act = gate / (1.0 + jnp.exp(-gate)) * up
        a_scr[rows, lo:lo + FF_CHUNK] = act.astype(BF16)


def _down_rows(rows, h, wd_ref, a_scr):
    return h + 0.5 * jnp.dot(a_scr[rows, :], wd_ref[...], preferred_element_type=F32)


def _project_rows(rows, win_ref, n_scr, proj_ref):
    for c in range(IN_WIDTH // PROJ_CHUNK):
        lo = c * PROJ_CHUNK
        p = jnp.dot(n_scr[rows, :], win_ref[:, lo:lo + PROJ_CHUNK], preferred_element_type=F32)
        if lo < SB_WIDTH:
            p = p * (Q_SCALE * LOG2E)
        elif 3 * SB_WIDTH <= lo < 3 * SB_WIDTH + SWA_WIDTH:
            p = p * Q_SCALE
        proj_ref[rows, lo:lo + PROJ_CHUNK] = p.astype(BF16)


HALF_A = slice(0, TOKEN_TILE)
HALF_B = slice(TOKEN_TILE, 2 * TOKEN_TILE)


def _ffn_in_kernel(h_ref, g1_ref, wgu_ref, wd_ref, gm_ref, win_ref, h_out_ref, proj_ref, n_scr, a_scr):
    _norm_rows(n_scr, HALF_A, h_ref[HALF_A, :], g1_ref)
    _gate_up_rows(HALF_A, wgu_ref, n_scr, a_scr)
    _norm_rows(n_scr, HALF_B, h_ref[HALF_B, :], g1_ref)
    h1a = _down_rows(HALF_A, h_ref[HALF_A, :], wd_ref, a_scr)
    h_out_ref[HALF_A, :] = h1a
    _gate_up_rows(HALF_B, wgu_ref, n_scr, a_scr)
    _norm_rows(n_scr, HALF_A, h_out_ref[HALF_A, :], gm_ref)
    h1b = _down_rows(HALF_B, h_ref[HALF_B, :], wd_ref, a_scr)
    h_out_ref[HALF_B, :] = h1b
    _project_rows(HALF_A, win_ref, n_scr, proj_ref)
    _norm_rows(n_scr, HALF_B, h_out_ref[HALF_B, :], gm_ref)
    _project_rows(HALF_B, win_ref, n_scr, proj_ref)


def _out_ffn_kernel(final_norm, osb_ref, osw_ref, h_ref, gsb_ref, gsw_ref, wout_ref, g2_ref, wgu_ref, wd_ref,
                    gf_ref, out_ref, n_scr, a_scr):
    def mix(rows):
        sb = _rms_norm(osb_ref[rows, :], gsb_ref[...]).astype(BF16)
        sw = _rms_norm(osw_ref[rows, :], gsw_ref[...]).astype(BF16)
        out_ref[rows, :] = (h_ref[rows, :]
                            + jnp.dot(sb, wout_ref[:SB_WIDTH, :], preferred_element_type=F32)
                            + jnp.dot(sw, wout_ref[SB_WIDTH:, :], preferred_element_type=F32))
        _norm_rows(n_scr, rows, out_ref[rows, :], g2_ref)

    def finish(rows):
        h3 = _down_rows(rows, out_ref[rows, :], wd_ref, a_scr)
        out_ref[rows, :] = _rms_norm(h3, gf_ref[...]) if final_norm else h3

    mix(HALF_A)
    _gate_up_rows(HALF_A, wgu_ref, n_scr, a_scr)
    mix(HALF_B)
    finish(HALF_A)
    _gate_up_rows(HALF_B, wgu_ref, n_scr, a_scr)
    finish(HALF_B)


def _sb_schedule(nq):
    nxt = list(range(nq))
    remaining = lambda a: nxt[a] + 1
    diag_slots, off_slots = [], []
    diag_left = list(range(nq - 1, -1, -1))
    while diag_left:
        a = diag_left.pop(0)
        nxt[a] -= 1
        ready = [b for b in range(nq) if b != a and 0 <= nxt[b] < b]
        if ready:
            b = max(ready, key=remaining)
            diag_slots.append((a, a, b, nxt[b], 0))
        else:
            b = diag_left.pop(0)
            diag_slots.append((a, a, b, b, 1))
        nxt[b] -= 1
    while any(n >= 0 for n in nxt):
        b1, b2 = sorted((b for b in range(nq) if nxt[b] >= 0), key=remaining, reverse=True)[:2]
        off_slots.append((b1, nxt[b1], b2, nxt[b2], 0))
        nxt[b1] -= 1
        nxt[b2] -= 1
    return diag_slots, off_slots


def _sb_kernel(n_diag_slots, n_off_slots, tbl_ref, q_ref, k_ref, v_ref, u_ref, o_ref, acc_ref, r_ref):
    T = SB_TILE
    lane = lax.broadcasted_iota(jnp.int32, (T, LANES), 1)
    row = lax.broadcasted_iota(jnp.int32, (T, T), 0)
    col = lax.broadcasted_iota(jnp.int32, (T, T), 1)
    causal = col < row
    head_lanes = [lane < HEAD_DIM, lane >= HEAD_DIM]

    acc_ref[...] = jnp.zeros_like(acc_ref)
    r_ref[...] = jnp.zeros_like(r_ref)

    class Slot:
        def __init__(self, index, masked):
            base = index * SLOT_FIELDS
            masks = [causal, causal | (tbl_ref[base + 4] == 0)] if masked else [None, None]
            self.items = []
            for n in range(2):
                qs = pl.multiple_of(tbl_ref[base + 2 * n] * T, T)
                ks = pl.multiple_of(tbl_ref[base + 2 * n + 1] * T, T)
                self.items += [(hh, qs, ks, masks[n]) for hh in range(2)]

        def scores(self):
            self.z = []
            for hh, qs, ks, mask in self.items:
                q = q_ref[pl.ds(qs, T), :]
                qh = jnp.where(head_lanes[hh], q, jnp.zeros_like(q))
                z = lax.dot_general(qh, k_ref[pl.ds(ks, T), :], (((1,), (1,)), ((), ())),
                                    preferred_element_type=F32)
                if mask is not None:
                    z = jnp.where(mask, z, NEG_INF)
                self.z.append(z)

        def softplus(self):
            self.lhs = []
            for z in self.z:
                neg_abs = lax.bitcast_convert_type(
                    lax.bitcast_convert_type(z, jnp.uint32) | jnp.uint32(0x80000000), F32)
                sp = jnp.maximum(z, 0.0) + jnp.log2(1.0 + jnp.exp2(neg_abs))
                self.lhs.append(sp.astype(BF16))

        def suffix_sums(self):
            self.incl = [jnp.dot(x, u_ref[...], preferred_element_type=F32) for x in self.lhs]

        def weights(self):
            self.w = []
            for (hh, qs, _, _), z, incl in zip(self.items, self.z, self.incl):
                r = r_ref[hh, pl.ds(qs, T), :]
                self.w.append(jnp.exp2(z - incl - jnp.concatenate([r, r], axis=1)).astype(BF16))
                r_ref[hh, pl.ds(qs, T), :] = r + jnp.broadcast_to(incl[:, 0:1], (T, LANES))

        def values(self):
            self.pv = [jnp.dot(w, v_ref[pl.ds(ks, T), :], preferred_element_type=F32)
                       for (_, _, ks, _), w in zip(self.items, self.w)]

        def accumulate(self):
            for (hh, qs, _, _), pv in zip(self.items, self.pv):
                acc_ref[hh, pl.ds(qs, T), :] += pv

    def slot_pair(a, b):
        a.scores()
        a.softplus()
        a.suffix_sums()
        b.scores()
        a.weights()
        a.values()
        b.softplus()
        a.accumulate()
        b.suffix_sums()
        b.weights()
        b.values()
        b.accumulate()

    def pair_loop(first, n_pairs, masked):
        def body(i, _):
            slot_pair(Slot(first + 2 * i, masked), Slot(first + 2 * i + 1, masked))
            return 0
        lax.fori_loop(0, n_pairs, body, 0)

    assert n_diag_slots % 2 == 1 and n_off_slots % 2 == 1
    pair_loop(0, n_diag_slots // 2, True)
    slot_pair(Slot(n_diag_slots - 1, True), Slot(n_diag_slots, False))
    pair_loop(n_diag_slots + 1, n_off_slots // 2, False)
    o_ref[...] = jnp.where(lax.broadcasted_iota(jnp.int32, o_ref.shape, 1) < HEAD_DIM, acc_ref[0], acc_ref[1])


def _swa_kernel(layer, sink_ref, q_ref, k_ref, v_ref, bias_ref, o_ref):
    n_blocks = q_ref.shape[0] // BLOCK
    n_slots = SWA_Q_HEADS // 2
    lane = lax.broadcasted_iota(jnp.int32, (BLOCK, LANES), 1)
    a = lax.broadcasted_iota(jnp.int32, (BLOCK, BLOCK), 0)
    c = lax.broadcasted_iota(jnp.int32, (BLOCK, BLOCK), 1)
    upper = c > a
    half = [lane < HEAD_DIM, lane >= HEAD_DIM]

    heads = [(p, sub) for p in range(n_slots) for sub in range(2)]

    class Block:
        def __init__(self, i):
            self.cur = pl.multiple_of(i * BLOCK, BLOCK)
            self.prev = pl.multiple_of(jnp.maximum(i - 1, 0) * BLOCK, BLOCK)
            self.table = 1 - jnp.minimum(i, 1)

        def rows(self, ref):
            return jnp.concatenate([ref[pl.ds(self.prev, BLOCK), :], ref[pl.ds(self.cur, BLOCK), :]], axis=0)

        def scores(self):
            kcat = self.rows(k_ref)
            self.s = []
            for p, sub in heads:
                q = q_ref[pl.ds(self.cur, BLOCK), p * LANES:(p + 1) * LANES]
                s = lax.dot_general(jnp.where(half[sub], q, jnp.zeros_like(q)), kcat, (((1,), (1,)), ((), ())),
                                    preferred_element_type=F32)
                self.s.append(jnp.where(upper, s[:, :BLOCK], s[:, BLOCK:]) + bias_ref[self.table, 2 * p + sub])

        def softmax(self):
            self.p = []
            for (p, sub), s in zip(heads, self.s):
                sink = sink_ref[layer, 2 * p + sub]
                m = jnp.maximum(jnp.max(s, axis=-1, keepdims=True), sink)
                e = jnp.exp(s - m)
                e = (e * (1.0 / (jnp.sum(e, axis=-1, keepdims=True) + jnp.exp(sink - m)))).astype(BF16)
                zero = jnp.zeros_like(e)
                self.p.append(jnp.concatenate([jnp.where(upper, e, zero), jnp.where(upper, zero, e)], axis=1))

        def values(self):
            vcat = self.rows(v_ref)
            self.o = [jnp.dot(pr, vcat, preferred_element_type=F32) for pr in self.p]

        def store(self):
            for p in range(n_slots):
                o_ref[pl.ds(self.cur, BLOCK), p * LANES:(p + 1) * LANES] = jnp.where(
                    half[0], self.o[2 * p], self.o[2 * p + 1])

    def block_pair(i, _):
        a, b = Block(2 * i), Block(2 * i + 1)
        a.scores()
        a.softmax()
        b.scores()
        a.values()
        b.softmax()
        a.store()
        b.values()
        b.store()
        return 0

    assert n_blocks % 2 == 0
    lax.fori_loop(0, n_blocks // 2, block_pair, 0)


def _t5_causal_bucket(dist):
    max_exact = N_BUCKETS // 2
    d = jnp.maximum(dist, 1).astype(F32)
    large = max_exact + (jnp.log(d / max_exact) / math.log(MAX_DISTANCE / max_exact)
                         * (N_BUCKETS - max_exact)).astype(jnp.int32)
    large = jnp.minimum(large, N_BUCKETS - 1)
    return jnp.where(dist < max_exact, dist, large)


def _swa_bias_table(rel_bias):
    period = 3 * BLOCK - 1
    dist = jnp.arange(period) - (BLOCK - 1)
    g = rel_bias.astype(F32)[_t5_causal_bucket(jnp.maximum(dist, 0))].T
    flat = jnp.tile(g, (1, BLOCK + 1))[:, :BLOCK * (period + 1)]
    return flat.reshape(SWA_Q_HEADS, BLOCK, period + 1)[:, :, :2 * BLOCK][:, :, ::-1]


def _params(*semantics):
    return pltpu.CompilerParams(dimension_semantics=semantics, vmem_limit_bytes=VMEM_LIMIT_BYTES)


def _ffn_in(layer, h, g1, wgu, wd, gm, win):
    n_tok = h.shape[0]
    tm = 2 * TOKEN_TILE
    row = lambda i: (i, 0)
    return pl.pallas_call(
        _ffn_in_kernel,
        out_shape=(jax.ShapeDtypeStruct((n_tok, D_MODEL), F32), jax.ShapeDtypeStruct((n_tok, IN_WIDTH), BF16)),
        grid=(n_tok // tm,),
        in_specs=[pl.BlockSpec((tm, D_MODEL), row), _layer_spec((1, D_MODEL), layer),
                  _layer_spec((D_MODEL, 2 * D_FF), layer), _layer_spec((D_FF, D_MODEL), layer),
                  _layer_spec((1, D_MODEL), layer), _layer_spec((D_MODEL, IN_WIDTH), layer)],
        out_specs=(pl.BlockSpec((tm, D_MODEL), row), pl.BlockSpec((tm, IN_WIDTH), row)),
        scratch_shapes=[pltpu.VMEM((tm, D_MODEL), BF16), pltpu.VMEM((tm, D_FF), BF16)],
        compiler_params=_params("parallel"),
        name="ffn_in",
    )(h, g1, wgu, wd, gm, win)


def _out_ffn(layer, o_sb, o_sw, h, gsb, gsw, wout, g2, wgu, wd, gf, final_norm):
    n_tok = h.shape[0]
    tm = 2 * TOKEN_TILE
    row = lambda i: (i, 0)
    return pl.pallas_call(
        functools.partial(_out_ffn_kernel, final_norm),
        out_shape=jax.ShapeDtypeStruct((n_tok, D_MODEL), F32),
        grid=(n_tok // tm,),
        in_specs=[pl.BlockSpec((tm, SB_WIDTH), row), pl.BlockSpec((tm, SWA_WIDTH), row),
                  pl.BlockSpec((tm, D_MODEL), row), _layer_spec((1, SB_WIDTH), layer),
                  _layer_spec((1, SWA_WIDTH), layer), _layer_spec((MIX_WIDTH, D_MODEL), layer),
                  _layer_spec((1, D_MODEL), layer), _layer_spec((D_MODEL, 2 * D_FF), layer),
                  _layer_spec((D_FF, D_MODEL), layer), _const_spec((1, D_MODEL))],
        out_specs=pl.BlockSpec((tm, D_MODEL), row),
        scratch_shapes=[pltpu.VMEM((tm, D_MODEL), BF16), pltpu.VMEM((tm, D_FF), BF16)],
        compiler_params=_params("parallel"),
        name="out_ffn",
    )(o_sb, o_sw, h, gsb, gsw, wout, g2, wgu, wd, gf)


def _sb_attention(proj, u2, batch, seq):
    T = SB_TILE
    pairs = SB_WIDTH // LANES
    diag_slots, off_slots = _sb_schedule(seq // T)
    table = jnp.asarray(np.array(diag_slots + off_slots, dtype=np.int32).reshape(-1))
    col_block = lambda c: (lambda b, p, tbl: (b, c * pairs + p))
    return pl.pallas_call(
        functools.partial(_sb_kernel, len(diag_slots), len(off_slots)),
        out_shape=jax.ShapeDtypeStruct((batch * seq, SB_WIDTH), F32),
        grid_spec=pltpu.PrefetchScalarGridSpec(
            num_scalar_prefetch=1,
            grid=(batch, pairs),
            in_specs=[pl.BlockSpec((seq, LANES), col_block(0)), pl.BlockSpec((seq, LANES), col_block(1)),
                      pl.BlockSpec((seq, LANES), col_block(2)),
                      pl.BlockSpec((T, T), lambda b, p, tbl: (0, 0), pipeline_mode=pl.Buffered(1))],
            out_specs=pl.BlockSpec((seq, LANES), col_block(0)),
            scratch_shapes=[pltpu.VMEM((2, seq, LANES), F32), pltpu.VMEM((2, seq, LANES), F32)]),
        compiler_params=_params("parallel", "parallel"),
        name="sb_attn",
    )(table, proj, proj, proj, u2)


def _swa_attention(layer, proj, sinks, bias, batch, seq):
    q_col = 3 * SB_WIDTH // SWA_WIDTH
    k_col = (3 * SB_WIDTH + SWA_WIDTH) // KV_WIDTH
    col_block = lambda col: (lambda b: (b, col))
    return pl.pallas_call(
        functools.partial(_swa_kernel, layer),
        out_shape=jax.ShapeDtypeStruct((batch * seq, SWA_WIDTH), F32),
        grid=(batch,),
        in_specs=[pl.BlockSpec(memory_space=pltpu.SMEM),
                  pl.BlockSpec((seq, SWA_WIDTH), col_block(q_col)),
                  pl.BlockSpec((seq, KV_WIDTH), col_block(k_col)), pl.BlockSpec((seq, KV_WIDTH), col_block(k_col + 1)),
                  _const_spec((2, SWA_Q_HEADS, BLOCK, BLOCK))],
        out_specs=pl.BlockSpec((seq, SWA_WIDTH), col_block(0)),
        compiler_params=_params("parallel"),
        name="swa_attn",
    )(sinks, proj, proj, proj, bias)


def _pair_swa_heads(x, axis):
    shape = x.shape
    k = shape[axis] // SWA_Q_HEADS
    x = x.reshape(shape[:axis] + (SWA_KV_HEADS, SWA_GROUP, k) + shape[axis + 1:])
    return jnp.swapaxes(x, axis, axis + 1).reshape(shape)


def _swa_folded_bias(rel_bias):
    table = _swa_bias_table(rel_bias)
    a = np.arange(BLOCK)[:, None]
    c = np.arange(BLOCK)[None, :]
    upper = jnp.asarray(c > a)
    normal = jnp.where(upper, table[:, :, :BLOCK], table[:, :, BLOCK:])
    first = jnp.where(upper, NEG_INF, table[:, :, BLOCK:])
    return _pair_swa_heads(jnp.stack([normal, first]), 1)


def kernel(x, norm_ffn1, w_ffn1_gu, w_ffn1_down, norm_mix, w_in, sinks, norm_out_sb, norm_out_swa, w_out,
           norm_ffn2, w_ffn2_gu, w_ffn2_down, rel_bias, norm_final):
    batch, seq, d_model = x.shape
    depth = w_in.shape[0]
    assert d_model == D_MODEL and seq % SB_TILE == 0 and (batch * seq) % (2 * TOKEN_TILE) == 0

    j = np.arange(SB_TILE)[:, None]
    s = np.arange(SB_TILE)[None, :]
    u2 = jnp.asarray(j >= s, dtype=BF16)
    bias = _swa_folded_bias(rel_bias)
    q_lo, q_hi = 3 * SB_WIDTH, 3 * SB_WIDTH + SWA_WIDTH

    gain = lambda g: g.astype(F32).reshape(depth, 1, -1)
    win = jnp.concatenate([w_in[:, :, :q_lo], _pair_swa_heads(w_in[:, :, q_lo:q_hi], 2), w_in[:, :, q_hi:]],
                          axis=2).astype(BF16)
    wout = jnp.concatenate([w_out[:, :SB_WIDTH], _pair_swa_heads(w_out[:, SB_WIDTH:], 1)], axis=1).astype(BF16)
    wgu1, wd1 = w_ffn1_gu.astype(BF16), w_ffn1_down.astype(BF16)
    wgu2, wd2 = w_ffn2_gu.astype(BF16), w_ffn2_down.astype(BF16)
    g1, gm, g2, gsb = gain(norm_ffn1), gain(norm_mix), gain(norm_ffn2), gain(norm_out_sb)
    gsw = gain(_pair_swa_heads(norm_out_swa, 1))
    gf = norm_final.astype(F32).reshape(1, -1)
    sinks_paired = _pair_swa_heads(sinks.astype(F32), 1)

    h = x.reshape(batch * seq, d_model)
    for l in range(depth):
        h, proj = _ffn_in(l, h, g1, wgu1, wd1, gm, win)
        o_sb = _sb_attention(proj, u2, batch, seq)
        o_sw = _swa_attention(l, proj, sinks_paired, bias, batch, seq)
        h = _out_ffn(l, o_sb, o_sw, h, gsb, gsw, wout, g2, wgu2, wd2, gf, final_norm=(l == depth - 1))
    return h.reshape(batch, seq, d_model)
```

```python
import functools
import math

import jax
import jax.numpy as jnp
import numpy as np
from jax import lax
from jax.experimental import pallas as pl
from jax.experimental.pallas import tpu as pltpu

D_MODEL = 1024
HEAD_DIM = 64
SB_HEADS = 8
SWA_Q_HEADS = 8
SWA_KV_HEADS = 2
SWA_GROUP = SWA_Q_HEADS // SWA_KV_HEADS
WINDOW = 128
BLOCK = 128
N_BUCKETS = 32
MAX_DISTANCE = 128
D_FF = 2816
EPS = 1e-6
NEG_INF = -1e30

SB_WIDTH = SB_HEADS * HEAD_DIM
SWA_WIDTH = SWA_Q_HEADS * HEAD_DIM
KV_WIDTH = SWA_KV_HEADS * HEAD_DIM
MIX_WIDTH = SB_WIDTH + SWA_WIDTH
IN_WIDTH = 3 * SB_WIDTH + SWA_WIDTH + 2 * KV_WIDTH
Q_SCALE = HEAD_DIM ** -0.5
LOG2E = math.log2(math.e)

LANES = 128
V7X_VMEM_BYTES = 64 * 1024 * 1024
VMEM_LIMIT_BYTES = 56 * 1024 * 1024

TOKEN_TILE = 256
FF_CHUNK = 256
PROJ_CHUNK = 256
SB_TILE = 256
SLOT_FIELDS = 5

F32 = jnp.float32
BF16 = jnp.bfloat16


def _rms_norm(x, g):
    return x * lax.rsqrt(jnp.mean(x * x, axis=-1, keepdims=True) + EPS) * g


def _const_spec(shape):
    return pl.BlockSpec(shape, lambda *_: (0,) * len(shape), pipeline_mode=pl.Buffered(1))


def _layer_spec(shape, layer):
    return pl.BlockSpec((None,) + shape, lambda *_: (layer,) + (0,) * len(shape), pipeline_mode=pl.Buffered(1))


def _norm_rows(n_scr, rows, x, g_ref):
    n_scr[rows, :] = _rms_norm(x, g_ref[...]).astype(BF16)


def _gate_up_rows(rows, wgu_ref, n_scr, a_scr):
    for c in range(D_FF // FF_CHUNK):
        lo = c * FF_CHUNK
        n = n_scr[rows, :]
        gate = jnp.dot(n, wgu_ref[:, lo:lo + FF_CHUNK], preferred_element_type=F32)
        up = jnp.dot(n, wgu_ref[:, D_FF + lo:D_FF + lo + FF_CHUNK], preferred_element_type=F32)
        act = gate / (1.0 + jnp.exp(-gate)) * up
        a_scr[rows, lo:lo + FF_CHUNK] = act.astype(BF16)


def _down_rows(rows, h, wd_ref, a_scr):
    return h + 0.5 * jnp.dot(a_scr[rows, :], wd_ref[...], preferred_element_type=F32)


def _project_rows(rows, win_ref, n_scr, proj_ref):
    for c in range(IN_WIDTH // PROJ_CHUNK):
        lo = c * PROJ_CHUNK
        p = jnp.dot(n_scr[rows, :], win_ref[:, lo:lo + PROJ_CHUNK], preferred_element_type=F32)
        if lo < SB_WIDTH:
            p = p * (Q_SCALE * LOG2E)
        elif 3 * SB_WIDTH <= lo < 3 * SB_WIDTH + SWA_WIDTH:
            p = p * Q_SCALE
        proj_ref[rows, lo:lo + PROJ_CHUNK] = p.astype(BF16)


HALF_A = slice(0, TOKEN_TILE)
HALF_B = slice(TOKEN_TILE, 2 * TOKEN_TILE)


def _ffn_in_kernel(h_ref, g1_ref, wgu_ref, wd_ref, gm_ref, win_ref, h_out_ref, proj_ref, n_scr, a_scr):
    _norm_rows(n_scr, HALF_A, h_ref[HALF_A, :], g1_ref)
    _gate_up_rows(HALF_A, wgu_ref, n_scr, a_scr)
    _norm_rows(n_scr, HALF_B, h_ref[HALF_B, :], g1_ref)
    h1a = _down_rows(HALF_A, h_ref[HALF_A, :], wd_ref, a_scr)
    h_out_ref[HALF_A, :] = h1a
    _gate_up_rows(HALF_B, wgu_ref, n_scr, a_scr)
    _norm_rows(n_scr, HALF_A, h_out_ref[HALF_A, :], gm_ref)
    h1b = _down_rows(HALF_B, h_ref[HALF_B, :], wd_ref, a_scr)
    h_out_ref[HALF_B, :] = h1b
    _project_rows(HALF_A, win_ref, n_scr, proj_ref)
    _norm_rows(n_scr, HALF_B, h_out_ref[HALF_B, :], gm_ref)
    _project_rows(HALF_B, win_ref, n_scr, proj_ref)


def _out_ffn_kernel(final_norm, osb_ref, osw_ref, h_ref, gsb_ref, gsw_ref, wout_ref, g2_ref, wgu_ref, wd_ref,
                    gf_ref, aliased_out_ref, out_ref, n_scr, a_scr):
    del aliased_out_ref

    def mix(rows):
        sb = _rms_norm(osb_ref[rows, :], gsb_ref[...]).astype(BF16)
        sw = _rms_norm(osw_ref[rows, :], gsw_ref[...]).astype(BF16)
        out_ref[rows, :] = (h_ref[rows, :]
                            + jnp.dot(sb, wout_ref[:SB_WIDTH, :], preferred_element_type=F32)
                            + jnp.dot(sw, wout_ref[SB_WIDTH:, :], preferred_element_type=F32))
        _norm_rows(n_scr, rows, out_ref[rows, :], g2_ref)

    def finish(rows):
        h3 = _down_rows(rows, out_ref[rows, :], wd_ref, a_scr)
        out_ref[rows, :] = _rms_norm(h3, gf_ref[...]) if final_norm else h3

    mix(HALF_A)
    _gate_up_rows(HALF_A, wgu_ref, n_scr, a_scr)
    mix(HALF_B)
    finish(HALF_A)
    _gate_up_rows(HALF_B, wgu_ref, n_scr, a_scr)
    finish(HALF_B)


def _sb_schedule(nq):
    nxt = list(range(nq))
    remaining = lambda a: nxt[a] + 1
    diag_slots, off_slots = [], []
    diag_left = list(range(nq - 1, -1, -1))
    while diag_left:
        a = diag_left.pop(0)
        nxt[a] -= 1
        ready = [b for b in range(nq) if b != a and 0 <= nxt[b] < b]
        if ready:
            b = max(ready, key=remaining)
            diag_slots.append((a, a, b, nxt[b], 0))
        else:
            b = diag_left.pop(0)
            diag_slots.append((a, a, b, b, 1))
        nxt[b] -= 1
    while any(n >= 0 for n in nxt):
        b1, b2 = sorted((b for b in range(nq) if nxt[b] >= 0), key=remaining, reverse=True)[:2]
        off_slots.append((b1, nxt[b1], b2, nxt[b2], 0))
        nxt[b1] -= 1
        nxt[b2] -= 1
    return diag_slots, off_slots


def _attn_ffn_kernel(n_diag_slots, n_off_slots, final_norm,
                     tbl_ref, q_ref, k_ref, v_ref, u_ref, osw_ref, h_ref, gsb_ref, gsw_ref, wout_ref, g2_ref,
                     wgu_ref, wd_ref, gf_ref, out_ref, osb_ref, acc_ref, r_ref, osb_scr, n_scr, f_scr):
    b = pl.program_id(0)
    t = pl.program_id(1)
    T = SB_TILE
    n_chunks = D_FF // FF_CHUNK
    seq = q_ref.shape[0]
    lane = lax.broadcasted_iota(jnp.int32, (T, LANES), 1)
    row = lax.broadcasted_iota(jnp.int32, (T, T), 0)
    col = lax.broadcasted_iota(jnp.int32, (T, T), 1)
    causal = col < row
    head_lanes = [lane < HEAD_DIM, lane >= HEAD_DIM]
    cur_buf = b & 1
    prev_buf = 1 - cur_buf

    class Slot:
        def __init__(self, index, masked):
            base = index * SLOT_FIELDS
            masks = [causal, causal | (tbl_ref[base + 4] == 0)] if masked else [None, None]
            self.items = []
            for n in range(2):
                qs = pl.multiple_of(tbl_ref[base + 2 * n] * T, T)
                ks = pl.multiple_of(tbl_ref[base + 2 * n + 1] * T, T)
                self.items += [(hh, qs, ks, masks[n]) for hh in range(2)]

        def scores(self):
            self.z = []
            for hh, qs, ks, mask in self.items:
                q = q_ref[pl.ds(qs, T), :]
                qh = jnp.where(head_lanes[hh], q, jnp.zeros_like(q))
                z = lax.dot_general(qh, k_ref[pl.ds(ks, T), :], (((1,), (1,)), ((), ())),
                                    preferred_element_type=F32)
                if mask is not None:
                    z = jnp.where(mask, z, NEG_INF)
                self.z.append(z)

        def softplus(self):
            self.lhs = []
            for z in self.z:
                neg_abs = lax.bitcast_convert_type(
                    lax.bitcast_convert_type(z, jnp.uint32) | jnp.uint32(0x80000000), F32)
                sp = jnp.maximum(z, 0.0) + jnp.log2(1.0 + jnp.exp2(neg_abs))
                self.lhs.append(sp.astype(BF16))

        def suffix_sums(self):
            self.incl = [jnp.dot(x, u_ref[...], preferred_element_type=F32) for x in self.lhs]

        def weights(self):
            self.w = []
            for (hh, qs, _, _), z, incl in zip(self.items, self.z, self.incl):
                r = r_ref[hh, pl.ds(qs, T), :]
                self.w.append(jnp.exp2(z - incl - jnp.concatenate([r, r], axis=1)).astype(BF16))
                r_ref[hh, pl.ds(qs, T), :] = r + jnp.broadcast_to(incl[:, 0:1], (T, LANES))

        def values(self):
            self.pv = [jnp.dot(w, v_ref[pl.ds(ks, T), :], preferred_element_type=F32)
                       for (_, _, ks, _), w in zip(self.items, self.w)]

        def accumulate(self):
            for (hh, qs, _, _), pv in zip(self.items, self.pv):
                acc_ref[hh, pl.ds(qs, T), :] += pv

    class FfnChunk:
        def __init__(self, c, rows):
            self.c, self.rows = c, rows

        def gate_up(self):
            n = n_scr[self.rows, :]
            self.gate = jnp.dot(n, wgu_ref[self.c], preferred_element_type=F32)
            self.up = jnp.dot(n, wgu_ref[n_chunks + self.c], preferred_element_type=F32)

        def down(self):
            act = (self.gate / (1.0 + jnp.exp(-self.gate)) * self.up).astype(BF16)
            f_scr[self.rows, :] += jnp.dot(act, wd_ref[self.c], preferred_element_type=F32)

    def slot_pair(sa, sb, chunk):
        ffn = [FfnChunk(chunk, rows) for rows in (HALF_A, HALF_B)] if chunk is not None else []
        sa.scores()
        for f in ffn[:1]:
            f.gate_up()
        sa.softplus()
        sa.suffix_sums()
        sb.scores()
        for f in ffn[:1]:
            f.down()
        sa.weights()
        sa.values()
        sb.softplus()
        for f in ffn[1:]:
            f.gate_up()
        sa.accumulate()
        sb.suffix_sums()
        sb.weights()
        for f in ffn[1:]:
            f.down()
        sb.values()
        sb.accumulate()

    def attention(with_ffn):
        assert n_diag_slots % 2 == 1 and n_off_slots % 2 == 1
        n_a, n_b = n_diag_slots // 2, n_off_slots // 2
        acc_ref[...] = jnp.zeros_like(acc_ref)
        r_ref[...] = jnp.zeros_like(r_ref)

        def pair_loop(first_slot, n_pairs, masked, first_chunk):
            def body(i, _):
                slot_pair(Slot(first_slot + 2 * i, masked), Slot(first_slot + 2 * i + 1, masked),
                          first_chunk + i if with_ffn else None)
                return 0
            lax.fori_loop(0, n_pairs, body, 0)

        pair_loop(0, n_a, True, 0)
        slot_pair(Slot(n_diag_slots - 1, True), Slot(n_diag_slots, False), n_a if with_ffn else None)
        pair_loop(n_diag_slots + 1, n_b, False, n_a + 1)
        o = jnp.where(lax.broadcasted_iota(jnp.int32, (seq, LANES), 1) < HEAD_DIM, acc_ref[0], acc_ref[1])
        osb_scr[cur_buf, t] = o
        osb_ref[...] = o
        return n_a + 1 + n_b

    def mix(rows):
        tile_rows = pl.ds(pl.multiple_of(t * (2 * TOKEN_TILE) + rows.start, TOKEN_TILE), TOKEN_TILE)
        osb = jnp.concatenate([osb_scr[prev_buf, p, tile_rows, :] for p in range(SB_WIDTH // LANES)], axis=1)
        sb = _rms_norm(osb, gsb_ref[...]).astype(BF16)
        sw = _rms_norm(osw_ref[rows, :], gsw_ref[...]).astype(BF16)
        out_ref[rows, :] = (h_ref[rows, :]
                            + jnp.dot(sb, wout_ref[:SB_WIDTH, :], preferred_element_type=F32)
                            + jnp.dot(sw, wout_ref[SB_WIDTH:, :], preferred_element_type=F32))
        _norm_rows(n_scr, rows, out_ref[rows, :], g2_ref)

    def finish(rows):
        h3 = out_ref[rows, :] + 0.5 * f_scr[rows, :]
        out_ref[rows, :] = _rms_norm(h3, gf_ref[...]) if final_norm else h3

    @pl.when(b == 0)
    def _():
        attention(with_ffn=False)
        out_ref[...] = jnp.zeros_like(out_ref)

    @pl.when(b > 0)
    def _():
        f_scr[...] = jnp.zeros_like(f_scr)
        mix(HALF_A)
        mix(HALF_B)
        chunks_done = attention(with_ffn=True)
        for c in range(chunks_done, n_chunks):
            for rows in (HALF_A, HALF_B):
                chunk = FfnChunk(c, rows)
                chunk.gate_up()
                chunk.down()
        finish(HALF_A)
        finish(HALF_B)


def _swa_kernel(layer, sink_ref, q_ref, k_ref, v_ref, bias_ref, o_ref):
    n_blocks = q_ref.shape[0] // BLOCK
    n_slots = SWA_Q_HEADS // 2
    lane = lax.broadcasted_iota(jnp.int32, (BLOCK, LANES), 1)
    a = lax.broadcasted_iota(jnp.int32, (BLOCK, BLOCK), 0)
    c = lax.broadcasted_iota(jnp.int32, (BLOCK, BLOCK), 1)
    upper = c > a
    half = [lane < HEAD_DIM, lane >= HEAD_DIM]

    heads = [(p, sub) for p in range(n_slots) for sub in range(2)]

    class Block:
        def __init__(self, i):
            self.cur = pl.multiple_of(i * BLOCK, BLOCK)
            self.prev = pl.multiple_of(jnp.maximum(i - 1, 0) * BLOCK, BLOCK)
            self.table = 1 - jnp.minimum(i, 1)

        def rows(self, ref):
            return jnp.concatenate([ref[pl.ds(self.prev, BLOCK), :], ref[pl.ds(self.cur, BLOCK), :]], axis=0)

        def scores(self):
            kcat = self.rows(k_ref)
            self.s = []
            for p, sub in heads:
                q = q_ref[pl.ds(self.cur, BLOCK), p * LANES:(p + 1) * LANES]
                s = lax.dot_general(jnp.where(half[sub], q, jnp.zeros_like(q)), kcat, (((1,), (1,)), ((), ())),
                                    preferred_element_type=F32)
                self.s.append(jnp.where(upper, s[:, :BLOCK], s[:, BLOCK:]) + bias_ref[self.table, 2 * p + sub])

        def softmax(self):
            self.p = []
            for (p, sub), s in zip(heads, self.s):
                sink = sink_ref[layer, 2 * p + sub]
                m = jnp.maximum(jnp.max(s, axis=-1, keepdims=True), sink)
                e = jnp.exp(s - m)
                e = (e * (1.0 / (jnp.sum(e, axis=-1, keepdims=True) + jnp.exp(sink - m)))).astype(BF16)
                zero = jnp.zeros_like(e)
                self.p.append(jnp.concatenate([jnp.where(upper, e, zero), jnp.where(upper, zero, e)], axis=1))

        def values(self):
            vcat = self.rows(v_ref)
            self.o = [jnp.dot(pr, vcat, preferred_element_type=F32) for pr in self.p]

        def store(self):
            for p in range(n_slots):
                o_ref[pl.ds(self.cur, BLOCK), p * LANES:(p + 1) * LANES] = jnp.where(
                    half[0], self.o[2 * p], self.o[2 * p + 1])

    def block_pair(i, _):
        a, b = Block(2 * i), Block(2 * i + 1)
        a.scores()
        a.softmax()
        b.scores()
        a.values()
        b.softmax()
        a.store()
        b.values()
        b.store()
        return 0

    assert n_blocks % 2 == 0
    lax.fori_loop(0, n_blocks // 2, block_pair, 0)


def _t5_causal_bucket(dist):
    max_exact = N_BUCKETS // 2
    d = jnp.maximum(dist, 1).astype(F32)
    large = max_exact + (jnp.log(d / max_exact) / math.log(MAX_DISTANCE / max_exact)
                         * (N_BUCKETS - max_exact)).astype(jnp.int32)
    large = jnp.minimum(large, N_BUCKETS - 1)
    return jnp.where(dist < max_exact, dist, large)


def _swa_bias_table(rel_bias):
    period = 3 * BLOCK - 1
    dist = jnp.arange(period) - (BLOCK - 1)
    g = rel_bias.astype(F32)[_t5_causal_bucket(jnp.maximum(dist, 0))].T
    flat = jnp.tile(g, (1, BLOCK + 1))[:, :BLOCK * (period + 1)]
    return flat.reshape(SWA_Q_HEADS, BLOCK, period + 1)[:, :, :2 * BLOCK][:, :, ::-1]


def _params(*semantics):
    return pltpu.CompilerParams(dimension_semantics=semantics, vmem_limit_bytes=VMEM_LIMIT_BYTES)


def _ffn_in(layer, h, g1, wgu, wd, gm, win):
    n_tok = h.shape[0]
    tm = 2 * TOKEN_TILE
    row = lambda i: (i, 0)
    return pl.pallas_call(
        _ffn_in_kernel,
        out_shape=(jax.ShapeDtypeStruct((n_tok, D_MODEL), F32), jax.ShapeDtypeStruct((n_tok, IN_WIDTH), BF16)),
        grid=(n_tok // tm,),
        in_specs=[pl.BlockSpec((tm, D_MODEL), row), _layer_spec((1, D_MODEL), layer),
                  _layer_spec((D_MODEL, 2 * D_FF), layer), _layer_spec((D_FF, D_MODEL), layer),
                  _layer_spec((1, D_MODEL), layer), _layer_spec((D_MODEL, IN_WIDTH), layer)],
        out_specs=(pl.BlockSpec((tm, D_MODEL), row), pl.BlockSpec((tm, IN_WIDTH), row)),
        scratch_shapes=[pltpu.VMEM((tm, D_MODEL), BF16), pltpu.VMEM((tm, D_FF), BF16)],
        compiler_params=_params("parallel"),
        name="ffn_in",
    )(h, g1, wgu, wd, gm, win)


def _attn_ffn(layer, proj, u2, o_sw, h, gsb, gsw, wout, g2, wgu_chunks, wd_chunks, gf, batch, seq, final_norm):
    T = SB_TILE
    tm = 2 * TOKEN_TILE
    pairs = SB_WIDTH // LANES
    tiles = seq // tm
    assert tiles == pairs, "one token tile of the previous batch element per head pair"
    n_chunks = D_FF // FF_CHUNK
    diag_slots, off_slots = _sb_schedule(seq // T)
    table = jnp.asarray(np.array(diag_slots + off_slots, dtype=np.int32).reshape(-1))
    col_block = lambda c: (lambda b, t, tbl: (b, c * pairs + t))
    prev_tile = lambda b, t, tbl: (jnp.maximum(b - 1, 0) * tiles + t, 0)
    out_tile = lambda b, t, tbl: (jnp.where(b == 0, batch - 1, b - 1) * tiles + t, 0)
    return pl.pallas_call(
        functools.partial(_attn_ffn_kernel, len(diag_slots), len(off_slots), final_norm),
        out_shape=(jax.ShapeDtypeStruct((batch * seq, D_MODEL), F32),
                   jax.ShapeDtypeStruct((batch * seq, SB_WIDTH), F32)),
        grid_spec=pltpu.PrefetchScalarGridSpec(
            num_scalar_prefetch=1,
            grid=(batch, pairs),
            in_specs=[pl.BlockSpec((seq, LANES), col_block(0)), pl.BlockSpec((seq, LANES), col_block(1)),
                      pl.BlockSpec((seq, LANES), col_block(2)), _const_spec((T, T)),
                      pl.BlockSpec((tm, SWA_WIDTH), prev_tile), pl.BlockSpec((tm, D_MODEL), prev_tile),
                      _layer_spec((1, SB_WIDTH), layer), _layer_spec((1, SWA_WIDTH), layer),
                      _layer_spec((MIX_WIDTH, D_MODEL), layer), _layer_spec((1, D_MODEL), layer),
                      _layer_spec((2 * n_chunks, D_MODEL, FF_CHUNK), layer),
                      _layer_spec((n_chunks, FF_CHUNK, D_MODEL), layer), _const_spec((1, D_MODEL))],
            out_specs=(pl.BlockSpec((tm, D_MODEL), out_tile), pl.BlockSpec((seq, LANES), lambda b, t, tbl: (b, t))),
            scratch_shapes=[pltpu.VMEM((2, seq, LANES), F32), pltpu.VMEM((2, seq, LANES), F32),
                            pltpu.VMEM((2, pairs, seq, LANES), F32), pltpu.VMEM((tm, D_MODEL), BF16),
                            pltpu.VMEM((tm, D_MODEL), F32)]),
        compiler_params=_params("arbitrary", "arbitrary"),
        name="attn_ffn",
    )(table, proj, proj, proj, u2, o_sw, h, gsb, gsw, wout, g2, wgu_chunks, wd_chunks, gf)


def _out_ffn_tail(layer, o_sb, o_sw, h, gsb, gsw, wout, g2, wgu, wd, gf, h_next, batch, seq, final_norm):
    tm = 2 * TOKEN_TILE
    tiles = seq // tm
    last = lambda i: ((batch - 1) * tiles + i, 0)
    return pl.pallas_call(
        functools.partial(_out_ffn_kernel, final_norm),
        out_shape=jax.ShapeDtypeStruct(h_next.shape, F32),
        grid=(tiles,),
        in_specs=[pl.BlockSpec((tm, SB_WIDTH), last), pl.BlockSpec((tm, SWA_WIDTH), last),
                  pl.BlockSpec((tm, D_MODEL), last), _layer_spec((1, SB_WIDTH), layer),
                  _layer_spec((1, SWA_WIDTH), layer), _layer_spec((MIX_WIDTH, D_MODEL), layer),
                  _layer_spec((1, D_MODEL), layer), _layer_spec((D_MODEL, 2 * D_FF), layer),
                  _layer_spec((D_FF, D_MODEL), layer), _const_spec((1, D_MODEL)),
                  pl.BlockSpec(memory_space=pl.ANY)],
        out_specs=pl.BlockSpec((tm, D_MODEL), last),
        scratch_shapes=[pltpu.VMEM((tm, D_MODEL), BF16), pltpu.VMEM((tm, D_FF), BF16)],
        input_output_aliases={10: 0},
        compiler_params=_params("parallel"),
        name="out_ffn_tail",
    )(o_sb, o_sw, h, gsb, gsw, wout, g2, wgu, wd, gf, h_next)


def _swa_attention(layer, proj, sinks, bias, batch, seq):
    q_col = 3 * SB_WIDTH // SWA_WIDTH
    k_col = (3 * SB_WIDTH + SWA_WIDTH) // KV_WIDTH
    col_block = lambda col: (lambda b: (b, col))
    return pl.pallas_call(
        functools.partial(_swa_kernel, layer),
        out_shape=jax.ShapeDtypeStruct((batch * seq, SWA_WIDTH), F32),
        grid=(batch,),
        in_specs=[pl.BlockSpec(memory_space=pltpu.SMEM),
                  pl.BlockSpec((seq, SWA_WIDTH), col_block(q_col)),
                  pl.BlockSpec((seq, KV_WIDTH), col_block(k_col)), pl.BlockSpec((seq, KV_WIDTH), col_block(k_col + 1)),
                  _const_spec((2, SWA_Q_HEADS, BLOCK, BLOCK))],
        out_specs=pl.BlockSpec((seq, SWA_WIDTH), col_block(0)),
        compiler_params=_params("parallel"),
        name="swa_attn",
    )(sinks, proj, proj, proj, bias)


def _pair_swa_heads(x, axis):
    shape = x.shape
    k = shape[axis] // SWA_Q_HEADS
    x = x.reshape(shape[:axis] + (SWA_KV_HEADS, SWA_GROUP, k) + shape[axis + 1:])
    return jnp.swapaxes(x, axis, axis + 1).reshape(shape)


def _swa_folded_bias(rel_bias):
    table = _swa_bias_table(rel_bias)
    a = np.arange(BLOCK)[:, None]
    c = np.arange(BLOCK)[None, :]
    upper = jnp.asarray(c > a)
    normal = jnp.where(upper, table[:, :, :BLOCK], table[:, :, BLOCK:])
    first = jnp.where(upper, NEG_INF, table[:, :, BLOCK:])
    return _pair_swa_heads(jnp.stack([normal, first]), 1)


def kernel(x, norm_ffn1, w_ffn1_gu, w_ffn1_down, norm_mix, w_in, sinks, norm_out_sb, norm_out_swa, w_out,
           norm_ffn2, w_ffn2_gu, w_ffn2_down, rel_bias, norm_final):
    batch, seq, d_model = x.shape
    depth = w_in.shape[0]
    assert d_model == D_MODEL and seq % SB_TILE == 0 and (batch * seq) % (2 * TOKEN_TILE) == 0

    j = np.arange(SB_TILE)[:, None]
    s = np.arange(SB_TILE)[None, :]
    u2 = jnp.asarray(j >= s, dtype=BF16)
    bias = _swa_folded_bias(rel_bias)
    q_lo, q_hi = 3 * SB_WIDTH, 3 * SB_WIDTH + SWA_WIDTH

    gain = lambda g: g.astype(F32).reshape(depth, 1, -1)
    win = jnp.concatenate([w_in[:, :, :q_lo], _pair_swa_heads(w_in[:, :, q_lo:q_hi], 2), w_in[:, :, q_hi:]],
                          axis=2).astype(BF16)
    wout = jnp.concatenate([w_out[:, :SB_WIDTH], _pair_swa_heads(w_out[:, SB_WIDTH:], 1)], axis=1).astype(BF16)
    wgu1, wd1 = w_ffn1_gu.astype(BF16), w_ffn1_down.astype(BF16)
    wgu2, wd2 = w_ffn2_gu.astype(BF16), w_ffn2_down.astype(BF16)
    g1, gm, g2, gsb = gain(norm_ffn1), gain(norm_mix), gain(norm_ffn2), gain(norm_out_sb)
    gsw = gain(_pair_swa_heads(norm_out_swa, 1))
    gf = norm_final.astype(F32).reshape(1, -1)
    sinks_paired = _pair_swa_heads(sinks.astype(F32), 1)

    n_chunks = D_FF // FF_CHUNK
    wgu2_chunks = jnp.swapaxes(wgu2.reshape(depth, D_MODEL, 2 * n_chunks, FF_CHUNK), 1, 2)
    wd2_chunks = wd2.reshape(depth, n_chunks, FF_CHUNK, D_MODEL)

    h = x.reshape(batch * seq, d_model)
    for l in range(depth):
        final = l == depth - 1
        h, proj = _ffn_in(l, h, g1, wgu1, wd1, gm, win)
        o_sw = _swa_attention(l, proj, sinks_paired, bias, batch, seq)
        h_next, o_sb = _attn_ffn(l, proj, u2, o_sw, h, gsb, gsw, wout, g2, wgu2_chunks, wd2_chunks, gf,
                                      batch, seq, final)
        h = _out_ffn_tail(l, o_sb, o_sw, h, gsb, gsw, wout, g2, wgu2, wd2, gf, h_next, batch, seq, final)
    return h.reshape(batch, seq, d_model)
```

```python
import functools
import math

import jax
import jax.numpy as jnp
import numpy as np
from jax import lax
from jax.experimental import pallas as pl
from jax.experimental.pallas import tpu as pltpu

D_MODEL = 1024
HEAD_DIM = 64
SB_HEADS = 8
SWA_Q_HEADS = 8
SWA_KV_HEADS = 2
SWA_GROUP = SWA_Q_HEADS // SWA_KV_HEADS
WINDOW = 128
BLOCK = 128
N_BUCKETS = 32
MAX_DISTANCE = 128
D_FF = 2816
EPS = 1e-6
NEG_INF = -1e30

SB_WIDTH = SB_HEADS * HEAD_DIM
SWA_WIDTH = SWA_Q_HEADS * HEAD_DIM
KV_WIDTH = SWA_KV_HEADS * HEAD_DIM
MIX_WIDTH = SB_WIDTH + SWA_WIDTH
IN_WIDTH = 3 * SB_WIDTH + SWA_WIDTH + 2 * KV_WIDTH
Q_SCALE = HEAD_DIM ** -0.5
LOG2E = math.log2(math.e)

LANES = 128
V7X_VMEM_BYTES = 64 * 1024 * 1024
VMEM_LIMIT_BYTES = 56 * 1024 * 1024

TOKEN_TILE = 256
FF_CHUNK = 256
PROJ_CHUNK = 256
SB_TILE = 256
SLOT_FIELDS = 5

F32 = jnp.float32
BF16 = jnp.bfloat16


def _rms_norm(x, g):
    return x * lax.rsqrt(jnp.mean(x * x, axis=-1, keepdims=True) + EPS) * g


def _const_spec(shape):
    return pl.BlockSpec(shape, lambda *_: (0,) * len(shape), pipeline_mode=pl.Buffered(1))


def _layer_spec(shape, layer):
    return pl.BlockSpec((None,) + shape, lambda *_: (layer,) + (0,) * len(shape), pipeline_mode=pl.Buffered(1))


def _norm_rows(n_scr, rows, x, g_ref):
    n_scr[rows, :] = _rms_norm(x, g_ref[...]).astype(BF16)


def _gate_up_rows(rows, wgu_ref, n_scr, a_scr):
    for c in range(D_FF // FF_CHUNK):
        lo = c * FF_CHUNK
        n = n_scr[rows, :]
        gate = jnp.dot(n, wgu_ref[:, lo:lo + FF_CHUNK], preferred_element_type=F32)
        up = jnp.dot(n, wgu_ref[:, D_FF + lo:D_FF + lo + FF_CHUNK], preferred_element_type=F32)
        act = gate / (1.0 + jnp.exp(-gate)) * up
        a_scr[rows, lo:lo + FF_CHUNK] = act.astype(BF16)


def _down_rows(rows, h, wd_ref, a_scr):
    return h + 0.5 * jnp.dot(a_scr[rows, :], wd_ref[...], preferred_element_type=F32)


def _project_rows(rows, win_ref, n_scr, proj_ref):
    for c in range(IN_WIDTH // PROJ_CHUNK):
        lo = c * PROJ_CHUNK
        p = jnp.dot(n_scr[rows, :], win_ref[:, lo:lo + PROJ_CHUNK], preferred_element_type=F32)
        if lo < SB_WIDTH:
            p = p * (Q_SCALE * LOG2E)
        elif 3 * SB_WIDTH <= lo < 3 * SB_WIDTH + SWA_WIDTH:
            p = p * Q_SCALE
        proj_ref[rows, lo:lo + PROJ_CHUNK] = p.astype(BF16)


HALF_A = slice(0, TOKEN_TILE)
HALF_B = slice(TOKEN_TILE, 2 * TOKEN_TILE)


def _ffn_in_kernel(h_ref, g1_ref, wgu_ref, wd_ref, gm_ref, win_ref, h_out_ref, proj_ref, n_scr, a_scr):
    _norm_rows(n_scr, HALF_A, h_ref[HALF_A, :], g1_ref)
    _gate_up_rows(HALF_A, wgu_ref, n_scr, a_scr)
    _norm_rows(n_scr, HALF_B, h_ref[HALF_B, :], g1_ref)
    h1a = _down_rows(HALF_A, h_ref[HALF_A, :], wd_ref, a_scr)
    h_out_ref[HALF_A, :] = h1a
    _gate_up_rows(HALF_B, wgu_ref, n_scr, a_scr)
    _norm_rows(n_scr, HALF_A, h_out_ref[HALF_A, :], gm_ref)
    h1b = _down_rows(HALF_B, h_ref[HALF_B, :], wd_ref, a_scr)
    h_out_ref[HALF_B, :] = h1b
    _project_rows(HALF_A, win_ref, n_scr, proj_ref)
    _norm_rows(n_scr, HALF_B, h_out_ref[HALF_B, :], gm_ref)
    _project_rows(HALF_B, win_ref, n_scr, proj_ref)


def _out_ffn_kernel(final_norm, osb_ref, osw_ref, h_ref, gsb_ref, gsw_ref, wout_ref, g2_ref, wgu_ref, wd_ref,
                    gf_ref, out_ref, n_scr, a_scr):
    def mix(rows):
        sb = _rms_norm(osb_ref[rows, :], gsb_ref[...]).astype(BF16)
        sw = _rms_norm(osw_ref[rows, :], gsw_ref[...]).astype(BF16)
        out_ref[rows, :] = (h_ref[rows, :]
                            + jnp.dot(sb, wout_ref[:SB_WIDTH, :], preferred_element_type=F32)
                            + jnp.dot(sw, wout_ref[SB_WIDTH:, :], preferred_element_type=F32))
        _norm_rows(n_scr, rows, out_ref[rows, :], g2_ref)

    def finish(rows):
        h3 = _down_rows(rows, out_ref[rows, :], wd_ref, a_scr)
        out_ref[rows, :] = _rms_norm(h3, gf_ref[...]) if final_norm else h3

    mix(HALF_A)
    _gate_up_rows(HALF_A, wgu_ref, n_scr, a_scr)
    mix(HALF_B)
    finish(HALF_A)
    _gate_up_rows(HALF_B, wgu_ref, n_scr, a_scr)
    finish(HALF_B)


def _sb_schedule(nq):
    nxt = list(range(nq))
    remaining = lambda a: nxt[a] + 1
    diag_slots, off_slots = [], []
    diag_left = list(range(nq - 1, -1, -1))
    while diag_left:
        a = diag_left.pop(0)
        nxt[a] -= 1
        ready = [b for b in range(nq) if b != a and 0 <= nxt[b] < b]
        if ready:
            b = max(ready, key=remaining)
            diag_slots.append((a, a, b, nxt[b], 0))
        else:
            b = diag_left.pop(0)
            diag_slots.append((a, a, b, b, 1))
        nxt[b] -= 1
    while any(n >= 0 for n in nxt):
        b1, b2 = sorted((b for b in range(nq) if nxt[b] >= 0), key=remaining, reverse=True)[:2]
        off_slots.append((b1, nxt[b1], b2, nxt[b2], 0))
        nxt[b1] -= 1
        nxt[b2] -= 1
    return diag_slots, off_slots


def _sb_kernel(n_diag_slots, n_off_slots, tbl_ref, q_ref, k_ref, v_ref, u_ref, o_ref, acc_ref, r_ref):
    T = SB_TILE
    lane = lax.broadcasted_iota(jnp.int32, (T, LANES), 1)
    row = lax.broadcasted_iota(jnp.int32, (T, T), 0)
    col = lax.broadcasted_iota(jnp.int32, (T, T), 1)
    causal = col < row
    head_lanes = [lane < HEAD_DIM, lane >= HEAD_DIM]

    acc_ref[...] = jnp.zeros_like(acc_ref)
    r_ref[...] = jnp.zeros_like(r_ref)

    class Slot:
        def __init__(self, index, masked):
            base = index * SLOT_FIELDS
            masks = [causal, causal | (tbl_ref[base + 4] == 0)] if masked else [None, None]
            self.items = []
            for n in range(2):
                qs = pl.multiple_of(tbl_ref[base + 2 * n] * T, T)
                ks = pl.multiple_of(tbl_ref[base + 2 * n + 1] * T, T)
                self.items += [(hh, qs, ks, masks[n]) for hh in range(2)]

        def scores(self):
            self.z = []
            for hh, qs, ks, mask in self.items:
                q = q_ref[pl.ds(qs, T), :]
                qh = jnp.where(head_lanes[hh], q, jnp.zeros_like(q))
                z = lax.dot_general(qh, k_ref[pl.ds(ks, T), :], (((1,), (1,)), ((), ())),
                                    preferred_element_type=F32)
                if mask is not None:
                    z = jnp.where(mask, z, NEG_INF)
                self.z.append(z)

        def softplus(self):
            self.lhs = []
            for z in self.z:
                zb = z.astype(BF16)
                neg_abs = lax.bitcast_convert_type(
                    lax.bitcast_convert_type(zb, jnp.uint16) | jnp.uint16(0x8000), BF16)
                log2_1p = jnp.log(1.0 + jnp.exp2(neg_abs)) * jnp.asarray(LOG2E, BF16)
                self.lhs.append(jnp.maximum(zb, 0.0) + log2_1p)

        def suffix_sums(self):
            self.incl = [jnp.dot(x, u_ref[...], preferred_element_type=F32) for x in self.lhs]

        def weights(self):
            self.w = []
            for (hh, qs, _, _), z, incl in zip(self.items, self.z, self.incl):
                r = r_ref[hh, pl.ds(qs, T), :]
                self.w.append(jnp.exp2(z - incl - jnp.concatenate([r, r], axis=1)).astype(BF16))
                r_ref[hh, pl.ds(qs, T), :] = r + jnp.broadcast_to(incl[:, 0:1], (T, LANES))

        def values(self):
            self.pv = [jnp.dot(w, v_ref[pl.ds(ks, T), :], preferred_element_type=F32)
                       for (_, _, ks, _), w in zip(self.items, self.w)]

        def accumulate(self):
            for (hh, qs, _, _), pv in zip(self.items, self.pv):
                acc_ref[hh, pl.ds(qs, T), :] += pv

    def slot_pair(a, b):
        a.scores()
        a.softplus()
        a.suffix_sums()
        b.scores()
        a.weights()
        a.values()
        b.softplus()
        a.accumulate()
        b.suffix_sums()
        b.weights()
        b.values()
        b.accumulate()

    def pair_loop(first, n_pairs, masked):
        def body(i, _):
            slot_pair(Slot(first + 2 * i, masked), Slot(first + 2 * i + 1, masked))
            return 0
        lax.fori_loop(0, n_pairs, body, 0)

    assert n_diag_slots % 2 == 1 and n_off_slots % 2 == 1
    pair_loop(0, n_diag_slots // 2, True)
    slot_pair(Slot(n_diag_slots - 1, True), Slot(n_diag_slots, False))
    pair_loop(n_diag_slots + 1, n_off_slots // 2, False)
    o_ref[...] = jnp.where(lax.broadcasted_iota(jnp.int32, o_ref.shape, 1) < HEAD_DIM, acc_ref[0], acc_ref[1])


def _swa_kernel(layer, sink_ref, q_ref, k_ref, v_ref, bias_ref, o_ref):
    n_blocks = q_ref.shape[0] // BLOCK
    n_slots = SWA_Q_HEADS // 2
    lane = lax.broadcasted_iota(jnp.int32, (BLOCK, LANES), 1)
    a = lax.broadcasted_iota(jnp.int32, (BLOCK, BLOCK), 0)
    c = lax.broadcasted_iota(jnp.int32, (BLOCK, BLOCK), 1)
    upper = c > a
    half = [lane < HEAD_DIM, lane >= HEAD_DIM]

    heads = [(p, sub) for p in range(n_slots) for sub in range(2)]

    class Block:
        def __init__(self, i):
            self.cur = pl.multiple_of(i * BLOCK, BLOCK)
            self.prev = pl.multiple_of(jnp.maximum(i - 1, 0) * BLOCK, BLOCK)
            self.table = 1 - jnp.minimum(i, 1)

        def rows(self, ref):
            return jnp.concatenate([ref[pl.ds(self.prev, BLOCK), :], ref[pl.ds(self.cur, BLOCK), :]], axis=0)

        def scores(self):
            kcat = self.rows(k_ref)
            self.s = []
            for p, sub in heads:
                q = q_ref[pl.ds(self.cur, BLOCK), p * LANES:(p + 1) * LANES]
                s = lax.dot_general(jnp.where(half[sub], q, jnp.zeros_like(q)), kcat, (((1,), (1,)), ((), ())),
                                    preferred_element_type=F32)
                self.s.append(jnp.where(upper, s[:, :BLOCK], s[:, BLOCK:]) + bias_ref[self.table, 2 * p + sub])

        def softmax(self):
            self.p = []
            for (p, sub), s in zip(heads, self.s):
                sink = sink_ref[layer, 2 * p + sub]
                m = jnp.maximum(jnp.max(s, axis=-1, keepdims=True), sink)
                e = jnp.exp(s - m)
                e = (e * (1.0 / (jnp.sum(e, axis=-1, keepdims=True) + jnp.exp(sink - m)))).astype(BF16)
                zero = jnp.zeros_like(e)
                self.p.append(jnp.concatenate([jnp.where(upper, e, zero), jnp.where(upper, zero, e)], axis=1))

        def values(self):
            vcat = self.rows(v_ref)
            self.o = [jnp.dot(pr, vcat, preferred_element_type=F32) for pr in self.p]

        def store(self):
            for p in range(n_slots):
                o_ref[pl.ds(self.cur, BLOCK), p * LANES:(p + 1) * LANES] = jnp.where(
                    half[0], self.o[2 * p], self.o[2 * p + 1])

    def block_pair(i, _):
        a, b = Block(2 * i), Block(2 * i + 1)
        a.scores()
        a.softmax()
        b.scores()
        a.values()
        b.softmax()
        a.store()
        b.values()
        b.store()
        return 0

    assert n_blocks % 2 == 0
    lax.fori_loop(0, n_blocks // 2, block_pair, 0)


def _t5_causal_bucket(dist):
    max_exact = N_BUCKETS // 2
    d = jnp.maximum(dist, 1).astype(F32)
    large = max_exact + (jnp.log(d / max_exact) / math.log(MAX_DISTANCE / max_exact)
                         * (N_BUCKETS - max_exact)).astype(jnp.int32)
    large = jnp.minimum(large, N_BUCKETS - 1)
    return jnp.where(dist < max_exact, dist, large)


def _swa_bias_table(rel_bias):
    period = 3 * BLOCK - 1
    dist = jnp.arange(period) - (BLOCK - 1)
    g = rel_bias.astype(F32)[_t5_causal_bucket(jnp.maximum(dist, 0))].T
    flat = jnp.tile(g, (1, BLOCK + 1))[:, :BLOCK * (period + 1)]
    return flat.reshape(SWA_Q_HEADS, BLOCK, period + 1)[:, :, :2 * BLOCK][:, :, ::-1]


def _params(*semantics):
    return pltpu.CompilerParams(dimension_semantics=semantics, vmem_limit_bytes=VMEM_LIMIT_BYTES)


def _ffn_in(layer, h, g1, wgu, wd, gm, win):
    n_tok = h.shape[0]
    tm = 2 * TOKEN_TILE
    row = lambda i: (i, 0)
    return pl.pallas_call(
        _ffn_in_kernel,
        out_shape=(jax.ShapeDtypeStruct((n_tok, D_MODEL), F32), jax.ShapeDtypeStruct((n_tok, IN_WIDTH), BF16)),
        grid=(n_tok // tm,),
        in_specs=[pl.BlockSpec((tm, D_MODEL), row), _layer_spec((1, D_MODEL), layer),
                  _layer_spec((D_MODEL, 2 * D_FF), layer), _layer_spec((D_FF, D_MODEL), layer),
                  _layer_spec((1, D_MODEL), layer), _layer_spec((D_MODEL, IN_WIDTH), layer)],
        out_specs=(pl.BlockSpec((tm, D_MODEL), row), pl.BlockSpec((tm, IN_WIDTH), row)),
        scratch_shapes=[pltpu.VMEM((tm, D_MODEL), BF16), pltpu.VMEM((tm, D_FF), BF16)],
        compiler_params=_params("parallel"),
        name="ffn_in",
    )(h, g1, wgu, wd, gm, win)


def _out_ffn(layer, o_sb, o_sw, h, gsb, gsw, wout, g2, wgu, wd, gf, final_norm):
    n_tok = h.shape[0]
    tm = 2 * TOKEN_TILE
    row = lambda i: (i, 0)
    return pl.pallas_call(
        functools.partial(_out_ffn_kernel, final_norm),
        out_shape=jax.ShapeDtypeStruct((n_tok, D_MODEL), F32),
        grid=(n_tok // tm,),
        in_specs=[pl.BlockSpec((tm, SB_WIDTH), row), pl.BlockSpec((tm, SWA_WIDTH), row),
                  pl.BlockSpec((tm, D_MODEL), row), _layer_spec((1, SB_WIDTH), layer),
                  _layer_spec((1, SWA_WIDTH), layer), _layer_spec((MIX_WIDTH, D_MODEL), layer),
                  _layer_spec((1, D_MODEL), layer), _layer_spec((D_MODEL, 2 * D_FF), layer),
                  _layer_spec((D_FF, D_MODEL), layer), _const_spec((1, D_MODEL))],
        out_specs=pl.BlockSpec((tm, D_MODEL), row),
        scratch_shapes=[pltpu.VMEM((tm, D_MODEL), BF16), pltpu.VMEM((tm, D_FF), BF16)],
        compiler_params=_params("parallel"),
        name="out_ffn",
    )(o_sb, o_sw, h, gsb, gsw, wout, g2, wgu, wd, gf)


def _sb_attention(proj, u2, batch, seq):
    T = SB_TILE
    pairs = SB_WIDTH // LANES
    diag_slots, off_slots = _sb_schedule(seq // T)
    table = jnp.asarray(np.array(diag_slots + off_slots, dtype=np.int32).reshape(-1))
    col_block = lambda c: (lambda b, p, tbl: (b, c * pairs + p))
    return pl.pallas_call(
        functools.partial(_sb_kernel, len(diag_slots), len(off_slots)),
        out_shape=jax.ShapeDtypeStruct((batch * seq, SB_WIDTH), F32),
        grid_spec=pltpu.PrefetchScalarGridSpec(
            num_scalar_prefetch=1,
            grid=(batch, pairs),
            in_specs=[pl.BlockSpec((seq, LANES), col_block(0)), pl.BlockSpec((seq, LANES), col_block(1)),
                      pl.BlockSpec((seq, LANES), col_block(2)),
                      pl.BlockSpec((T, T), lambda b, p, tbl: (0, 0), pipeline_mode=pl.Buffered(1))],
            out_specs=pl.BlockSpec((seq, LANES), col_block(0)),
            scratch_shapes=[pltpu.VMEM((2, seq, LANES), F32), pltpu.VMEM((2, seq, LANES), F32)]),
        compiler_params=_params("parallel", "parallel"),
        name="sb_attn",
    )(table, proj, proj, proj, u2)


def _swa_attention(layer, proj, sinks, bias, batch, seq):
    q_col = 3 * SB_WIDTH // SWA_WIDTH
    k_col = (3 * SB_WIDTH + SWA_WIDTH) // KV_WIDTH
    col_block = lambda col: (lambda b: (b, col))
    return pl.pallas_call(
        functools.partial(_swa_kernel, layer),
        out_shape=jax.ShapeDtypeStruct((batch * seq, SWA_WIDTH), F32),
        grid=(batch,),
        in_specs=[pl.BlockSpec(memory_space=pltpu.SMEM),
                  pl.BlockSpec((seq, SWA_WIDTH), col_block(q_col)),
                  pl.BlockSpec((seq, KV_WIDTH), col_block(k_col)), pl.BlockSpec((seq, KV_WIDTH), col_block(k_col + 1)),
                  _const_spec((2, SWA_Q_HEADS, BLOCK, BLOCK))],
        out_specs=pl.BlockSpec((seq, SWA_WIDTH), col_block(0)),
        compiler_params=_params("parallel"),
        name="swa_attn",
    )(sinks, proj, proj, proj, bias)


def _pair_swa_heads(x, axis):
    shape = x.shape
    k = shape[axis] // SWA_Q_HEADS
    x = x.reshape(shape[:axis] + (SWA_KV_HEADS, SWA_GROUP, k) + shape[axis + 1:])
    return jnp.swapaxes(x, axis, axis + 1).reshape(shape)


def _swa_folded_bias(rel_bias):
    table = _swa_bias_table(rel_bias)
    a = np.arange(BLOCK)[:, None]
    c = np.arange(BLOCK)[None, :]
    upper = jnp.asarray(c > a)
    normal = jnp.where(upper, table[:, :, :BLOCK], table[:, :, BLOCK:])
    first = jnp.where(upper, NEG_INF, table[:, :, BLOCK:])
    return _pair_swa_heads(jnp.stack([normal, first]), 1)


def kernel(x, norm_ffn1, w_ffn1_gu, w_ffn1_down, norm_mix, w_in, sinks, norm_out_sb, norm_out_swa, w_out,
           norm_ffn2, w_ffn2_gu, w_ffn2_down, rel_bias, norm_final):
    batch, seq, d_model = x.shape
    depth = w_in.shape[0]
    assert d_model == D_MODEL and seq % SB_TILE == 0 and (batch * seq) % (2 * TOKEN_TILE) == 0

    j = np.arange(SB_TILE)[:, None]
    s = np.arange(SB_TILE)[None, :]
    u2 = jnp.asarray(j >= s, dtype=BF16)
    bias = _swa_folded_bias(rel_bias)
    q_lo, q_hi = 3 * SB_WIDTH, 3 * SB_WIDTH + SWA_WIDTH

    gain = lambda g: g.astype(F32).reshape(depth, 1, -1)
    win = jnp.concatenate([w_in[:, :, :q_lo], _pair_swa_heads(w_in[:, :, q_lo:q_hi], 2), w_in[:, :, q_hi:]],
                          axis=2).astype(BF16)
    wout = jnp.concatenate([w_out[:, :SB_WIDTH], _pair_swa_heads(w_out[:, SB_WIDTH:], 1)], axis=1).astype(BF16)
    wgu1, wd1 = w_ffn1_gu.astype(BF16), w_ffn1_down.astype(BF16)
    wgu2, wd2 = w_ffn2_gu.astype(BF16), w_ffn2_down.astype(BF16)
    g1, gm, g2, gsb = gain(norm_ffn1), gain(norm_mix), gain(norm_ffn2), gain(norm_out_sb)
    gsw = gain(_pair_swa_heads(norm_out_swa, 1))
    gf = norm_final.astype(F32).reshape(1, -1)
    sinks_paired = _pair_swa_heads(sinks.astype(F32), 1)

    h = x.reshape(batch * seq, d_model)
    for l in range(depth):
        h, proj = _ffn_in(l, h, g1, wgu1, wd1, gm, win)
        o_sb = _sb_attention(proj, u2, batch, seq)
        o_sw = _swa_attention(l, proj, sinks_paired, bias, batch, seq)
        h = _out_ffn(l, o_sb, o_sw, h, gsb, gsw, wout, g2, wgu2, wd2, gf, final_norm=(l == depth - 1))
    return h.reshape(batch, seq, d_model)
```

```python
import functools
import math

import jax
import jax.numpy as jnp
import numpy as np
from jax import lax
from jax.experimental import pallas as pl
from jax.experimental.pallas import tpu as pltpu

D_MODEL = 1024
HEAD_DIM = 64
SB_HEADS = 8
SWA_Q_HEADS = 8
SWA_KV_HEADS = 2
SWA_GROUP = SWA_Q_HEADS // SWA_KV_HEADS
WINDOW = 128
BLOCK = 128
N_BUCKETS = 32
MAX_DISTANCE = 128
D_FF = 2816
EPS = 1e-6
NEG_INF = -1e30

SB_WIDTH = SB_HEADS * HEAD_DIM
SWA_WIDTH = SWA_Q_HEADS * HEAD_DIM
KV_WIDTH = SWA_KV_HEADS * HEAD_DIM
MIX_WIDTH = SB_WIDTH + SWA_WIDTH
IN_WIDTH = 3 * SB_WIDTH + SWA_WIDTH + 2 * KV_WIDTH
Q_SCALE = HEAD_DIM ** -0.5
LOG2E = math.log2(math.e)

LANES = 128
V7X_VMEM_BYTES = 64 * 1024 * 1024
VMEM_LIMIT_BYTES = 56 * 1024 * 1024

TOKEN_TILE = 256
FF_CHUNK = 256
PROJ_CHUNK = 256
SB_TILE = 256
SLOT_FIELDS = 5
SLOT_GROUP = 6

F32 = jnp.float32
BF16 = jnp.bfloat16


def _rms_norm(x, g):
    return x * lax.rsqrt(jnp.mean(x * x, axis=-1, keepdims=True) + EPS) * g


def _const_spec(shape):
    return pl.BlockSpec(shape, lambda *_: (0,) * len(shape), pipeline_mode=pl.Buffered(1))


def _layer_spec(shape, layer):
    return pl.BlockSpec((None,) + shape, lambda *_: (layer,) + (0,) * len(shape), pipeline_mode=pl.Buffered(1))


def _norm_rows(n_scr, rows, x, g_ref):
    n_scr[rows, :] = _rms_norm(x, g_ref[...]).astype(BF16)


def _gate_up_rows(rows, wgu_ref, n_scr, a_scr):
    for c in range(D_FF // FF_CHUNK):
        lo = c * FF_CHUNK
        n = n_scr[rows, :]
        gate = jnp.dot(n, wgu_ref[:, lo:lo + FF_CHUNK], preferred_element_type=F32)
        up = jnp.dot(n, wgu_ref[:, D_FF + lo:D_FF + lo + FF_CHUNK], preferred_element_type=F32)
        act = gate / (1.0 + jnp.exp(-gate)) * up
        a_scr[rows, lo:lo + FF_CHUNK] = act.astype(BF16)


def _down_rows(rows, h, wd_ref, a_scr):
    return h + 0.5 * jnp.dot(a_scr[rows, :], wd_ref[...], preferred_element_type=F32)


def _project_rows(rows, win_ref, n_scr, proj_ref):
    for c in range(IN_WIDTH // PROJ_CHUNK):
        lo = c * PROJ_CHUNK
        p = jnp.dot(n_scr[rows, :], win_ref[:, lo:lo + PROJ_CHUNK], preferred_element_type=F32)
        if lo < SB_WIDTH:
            p = p * (Q_SCALE * LOG2E)
        elif 3 * SB_WIDTH <= lo < 3 * SB_WIDTH + SWA_WIDTH:
            p = p * Q_SCALE
        proj_ref[rows, lo:lo + PROJ_CHUNK] = p.astype(BF16)


HALF_A = slice(0, TOKEN_TILE)
HALF_B = slice(TOKEN_TILE, 2 * TOKEN_TILE)


def _ffn_in_kernel(h_ref, g1_ref, wgu_ref, wd_ref, gm_ref, win_ref, h_out_ref, proj_ref, n_scr, a_scr):
    _norm_rows(n_scr, HALF_A, h_ref[HALF_A, :], g1_ref)
    _gate_up_rows(HALF_A, wgu_ref, n_scr, a_scr)
    _norm_rows(n_scr, HALF_B, h_ref[HALF_B, :], g1_ref)
    h1a = _down_rows(HALF_A, h_ref[HALF_A, :], wd_ref, a_scr)
    h_out_ref[HALF_A, :] = h1a
    _gate_up_rows(HALF_B, wgu_ref, n_scr, a_scr)
    _norm_rows(n_scr, HALF_A, h_out_ref[HALF_A, :], gm_ref)
    h1b = _down_rows(HALF_B, h_ref[HALF_B, :], wd_ref, a_scr)
    h_out_ref[HALF_B, :] = h1b
    _project_rows(HALF_A, win_ref, n_scr, proj_ref)
    _norm_rows(n_scr, HALF_B, h_out_ref[HALF_B, :], gm_ref)
    _project_rows(HALF_B, win_ref, n_scr, proj_ref)


def _out_ffn_kernel(final_norm, osb_ref, osw_ref, h_ref, gsb_ref, gsw_ref, wout_ref, g2_ref, wgu_ref, wd_ref,
                    gf_ref, out_ref, n_scr, a_scr):
    def mix(rows):
        sb = _rms_norm(osb_ref[rows, :], gsb_ref[...]).astype(BF16)
        sw = _rms_norm(osw_ref[rows, :], gsw_ref[...]).astype(BF16)
        out_ref[rows, :] = (h_ref[rows, :]
                            + jnp.dot(sb, wout_ref[:SB_WIDTH, :], preferred_element_type=F32)
                            + jnp.dot(sw, wout_ref[SB_WIDTH:, :], preferred_element_type=F32))
        _norm_rows(n_scr, rows, out_ref[rows, :], g2_ref)

    def finish(rows):
        h3 = _down_rows(rows, out_ref[rows, :], wd_ref, a_scr)
        out_ref[rows, :] = _rms_norm(h3, gf_ref[...]) if final_norm else h3

    mix(HALF_A)
    _gate_up_rows(HALF_A, wgu_ref, n_scr, a_scr)
    mix(HALF_B)
    finish(HALF_A)
    _gate_up_rows(HALF_B, wgu_ref, n_scr, a_scr)
    finish(HALF_B)


def _sb_schedule(nq):
    nxt = list(range(nq))
    remaining = lambda a: nxt[a] + 1
    diag_slots, off_slots = [], []
    diag_left = list(range(nq - 1, -1, -1))
    while diag_left:
        a = diag_left.pop(0)
        nxt[a] -= 1
        ready = [b for b in range(nq) if b != a and 0 <= nxt[b] < b]
        if ready:
            b = max(ready, key=remaining)
            diag_slots.append((a, a, b, nxt[b], 0))
        else:
            b = diag_left.pop(0)
            diag_slots.append((a, a, b, b, 1))
        nxt[b] -= 1
    while any(n >= 0 for n in nxt):
        b1, b2 = sorted((b for b in range(nq) if nxt[b] >= 0), key=remaining, reverse=True)[:2]
        off_slots.append((b1, nxt[b1], b2, nxt[b2], 0))
        nxt[b1] -= 1
        nxt[b2] -= 1
    return diag_slots, off_slots


def _sb_kernel(n_diag_slots, n_off_slots, init_tiles, tbl_ref, q_ref, k_ref, v_ref, u_ref, o_ref, acc_ref, r_ref):
    T = SB_TILE
    H = T // 2
    iota = lambda shape, axis: lax.broadcasted_iota(jnp.int32, shape, axis)
    causal = iota((T, T), 1) < iota((T, T), 0)
    causal_upper = iota((H, H), 1) < iota((H, H), 0)
    causal_lower = iota((H, T), 1) < iota((H, T), 0) + H
    head_lanes = {n: [iota((n, LANES), 1) < HEAD_DIM, iota((n, LANES), 1) >= HEAD_DIM] for n in (H, T)}

    for qt in init_tiles:
        acc_ref[:, qt * T:(qt + 1) * T, :] = jnp.zeros((2, T, LANES), F32)
        r_ref[:, qt * T:(qt + 1) * T, :] = jnp.zeros((2, T, LANES), F32)

    class Slot:
        def __init__(self, index, masked):
            base = index * SLOT_FIELDS
            self.items = []
            for n in range(2):
                qs = pl.multiple_of(tbl_ref[base + 2 * n] * T, T)
                ks = pl.multiple_of(tbl_ref[base + 2 * n + 1] * T, T)
                for hh in range(2):
                    if masked and n == 0:
                        self.items.append((hh, pl.ds(qs, H), pl.ds(ks, H), causal_upper, True))
                        self.items.append((hh, pl.ds(pl.multiple_of(qs + H, H), H), pl.ds(ks, T), causal_lower, True))
                    else:
                        mask = causal | (tbl_ref[base + 4] == 0) if masked else None
                        self.items.append((hh, pl.ds(qs, T), pl.ds(ks, T), mask, False))

        def scores(self):
            self.z = []
            for hh, q_rows, k_rows, mask, _ in self.items:
                q = q_ref[q_rows, :]
                qh = jnp.where(head_lanes[q.shape[0]][hh], q, jnp.zeros_like(q))
                z = lax.dot_general(qh, k_ref[k_rows, :], (((1,), (1,)), ((), ())), preferred_element_type=F32)
                if mask is not None:
                    z = jnp.where(mask, z, NEG_INF)
                self.z.append(z)

        def softplus(self):
            self.lhs = []
            for z in self.z:
                zb = z.astype(BF16)
                neg_abs = lax.bitcast_convert_type(
                    lax.bitcast_convert_type(zb, jnp.uint16) | jnp.uint16(0x8000), BF16)
                log2_1p = jnp.log(1.0 + jnp.exp2(neg_abs)) * jnp.asarray(LOG2E, BF16)
                self.lhs.append(jnp.maximum(zb, 0.0) + log2_1p)

        def suffix_sums(self):
            self.incl = [jnp.dot(x, u_ref[:x.shape[1], :x.shape[1]], preferred_element_type=F32) for x in self.lhs]

        def weights(self):
            self.w = []
            for (hh, q_rows, _, _, first), z, incl in zip(self.items, self.z, self.incl):
                row_sum = jnp.broadcast_to(incl[:, 0:1], (z.shape[0], LANES))
                if first:
                    self.w.append(jnp.exp2(z - incl).astype(BF16))
                    r_ref[hh, q_rows, :] = row_sum
                else:
                    r = r_ref[hh, q_rows, :]
                    self.w.append(jnp.exp2(z - incl - jnp.concatenate([r] * (z.shape[1] // LANES), axis=1)).astype(BF16))
                    r_ref[hh, q_rows, :] = r + row_sum

        def values(self):
            self.pv = [jnp.dot(w, v_ref[k_rows, :], preferred_element_type=F32)
                       for (_, _, k_rows, _, _), w in zip(self.items, self.w)]

        def accumulate(self):
            for (hh, q_rows, _, _, first), pv in zip(self.items, self.pv):
                if first:
                    acc_ref[hh, q_rows, :] = pv
                else:
                    acc_ref[hh, q_rows, :] += pv

    def staggered(slots):
        for prev, cur in zip([None] + slots, slots + [None]):
            if cur is not None:
                cur.scores()
            if prev is not None:
                prev.weights()
                prev.values()
            if cur is not None:
                cur.softplus()
            if prev is not None:
                prev.accumulate()
            if cur is not None:
                cur.suffix_sums()

    def group_loop(first, n_groups, masked):
        def body(i, _):
            staggered([Slot(first + SLOT_GROUP * i + n, masked) for n in range(SLOT_GROUP)])
            return 0
        lax.fori_loop(0, n_groups, body, 0)

    n_diag_groups = n_diag_slots // SLOT_GROUP
    n_mixed_diag = n_diag_slots % SLOT_GROUP
    n_mixed_off = (SLOT_GROUP - n_mixed_diag) % SLOT_GROUP
    assert (n_off_slots - n_mixed_off) % SLOT_GROUP == 0
    group_loop(0, n_diag_groups, True)
    if n_mixed_diag:
        staggered([Slot(n_diag_groups * SLOT_GROUP + n, True) for n in range(n_mixed_diag)]
                  + [Slot(n_diag_slots + n, False) for n in range(n_mixed_off)])
    group_loop(n_diag_slots + n_mixed_off, (n_off_slots - n_mixed_off) // SLOT_GROUP, False)
    o_ref[...] = jnp.where(lax.broadcasted_iota(jnp.int32, o_ref.shape, 1) < HEAD_DIM, acc_ref[0], acc_ref[1])


def _swa_kernel(layer, sink_ref, q_ref, k_ref, v_ref, bias_ref, o_ref):
    n_blocks = q_ref.shape[0] // BLOCK
    n_slots = SWA_Q_HEADS // 2
    lane = lax.broadcasted_iota(jnp.int32, (BLOCK, LANES), 1)
    a = lax.broadcasted_iota(jnp.int32, (BLOCK, BLOCK), 0)
    c = lax.broadcasted_iota(jnp.int32, (BLOCK, BLOCK), 1)
    upper = c > a
    half = [lane < HEAD_DIM, lane >= HEAD_DIM]

    heads = [(p, sub) for p in range(n_slots) for sub in range(2)]

    class Block:
        def __init__(self, i):
            self.cur = pl.multiple_of(i * BLOCK, BLOCK)
            self.prev = pl.multiple_of(jnp.maximum(i - 1, 0) * BLOCK, BLOCK)
            self.table = 1 - jnp.minimum(i, 1)

        def rows(self, ref):
            return jnp.concatenate([ref[pl.ds(self.prev, BLOCK), :], ref[pl.ds(self.cur, BLOCK), :]], axis=0)

        def scores(self):
            kcat = self.rows(k_ref)
            self.s = []
            for p, sub in heads:
                q = q_ref[pl.ds(self.cur, BLOCK), p * LANES:(p + 1) * LANES]
                s = lax.dot_general(jnp.where(half[sub], q, jnp.zeros_like(q)), kcat, (((1,), (1,)), ((), ())),
                                    preferred_element_type=F32)
                self.s.append(jnp.where(upper, s[:, :BLOCK], s[:, BLOCK:]) + bias_ref[self.table, 2 * p + sub])

        def softmax(self):
            self.p = []
            for (p, sub), s in zip(heads, self.s):
                sink = sink_ref[layer, 2 * p + sub]
                m = jnp.maximum(jnp.max(s, axis=-1, keepdims=True), sink)
                e = jnp.exp(s - m)
                e = (e * (1.0 / (jnp.sum(e, axis=-1, keepdims=True) + jnp.exp(sink - m)))).astype(BF16)
                zero = jnp.zeros_like(e)
                self.p.append(jnp.concatenate([jnp.where(upper, e, zero), jnp.where(upper, zero, e)], axis=1))

        def values(self):
            vcat = self.rows(v_ref)
            self.o = [jnp.dot(pr, vcat, preferred_element_type=F32) for pr in self.p]

        def store(self):
            for p in range(n_slots):
                o_ref[pl.ds(self.cur, BLOCK), p * LANES:(p + 1) * LANES] = jnp.where(
                    half[0], self.o[2 * p], self.o[2 * p + 1])

    def block_pair(i, _):
        a, b = Block(2 * i), Block(2 * i + 1)
        a.scores()
        a.softmax()
        b.scores()
        a.values()
        b.softmax()
        a.store()
        b.values()
        b.store()
        return 0

    assert n_blocks % 2 == 0
    lax.fori_loop(0, n_blocks // 2, block_pair, 0)


def _t5_causal_bucket(dist):
    max_exact = N_BUCKETS // 2
    d = jnp.maximum(dist, 1).astype(F32)
    large = max_exact + (jnp.log(d / max_exact) / math.log(MAX_DISTANCE / max_exact)
                         * (N_BUCKETS - max_exact)).astype(jnp.int32)
    large = jnp.minimum(large, N_BUCKETS - 1)
    return jnp.where(dist < max_exact, dist, large)


def _swa_bias_table(rel_bias):
    period = 3 * BLOCK - 1
    dist = jnp.arange(period) - (BLOCK - 1)
    g = rel_bias.astype(F32)[_t5_causal_bucket(jnp.maximum(dist, 0))].T
    flat = jnp.tile(g, (1, BLOCK + 1))[:, :BLOCK * (period + 1)]
    return flat.reshape(SWA_Q_HEADS, BLOCK, period + 1)[:, :, :2 * BLOCK][:, :, ::-1]


def _params(*semantics):
    return pltpu.CompilerParams(dimension_semantics=semantics, vmem_limit_bytes=VMEM_LIMIT_BYTES)


def _ffn_in(layer, h, g1, wgu, wd, gm, win):
    n_tok = h.shape[0]
    tm = 2 * TOKEN_TILE
    row = lambda i: (i, 0)
    return pl.pallas_call(
        _ffn_in_kernel,
        out_shape=(jax.ShapeDtypeStruct((n_tok, D_MODEL), F32), jax.ShapeDtypeStruct((n_tok, IN_WIDTH), BF16)),
        grid=(n_tok // tm,),
        in_specs=[pl.BlockSpec((tm, D_MODEL), row), _layer_spec((1, D_MODEL), layer),
                  _layer_spec((D_MODEL, 2 * D_FF), layer), _layer_spec((D_FF, D_MODEL), layer),
                  _layer_spec((1, D_MODEL), layer), _layer_spec((D_MODEL, IN_WIDTH), layer)],
        out_specs=(pl.BlockSpec((tm, D_MODEL), row), pl.BlockSpec((tm, IN_WIDTH), row)),
        scratch_shapes=[pltpu.VMEM((tm, D_MODEL), BF16), pltpu.VMEM((tm, D_FF), BF16)],
        compiler_params=_params("parallel"),
        name="ffn_in",
    )(h, g1, wgu, wd, gm, win)


def _out_ffn(layer, o_sb, o_sw, h, gsb, gsw, wout, g2, wgu, wd, gf, final_norm):
    n_tok = h.shape[0]
    tm = 2 * TOKEN_TILE
    row = lambda i: (i, 0)
    return pl.pallas_call(
        functools.partial(_out_ffn_kernel, final_norm),
        out_shape=jax.ShapeDtypeStruct((n_tok, D_MODEL), F32),
        grid=(n_tok // tm,),
        in_specs=[pl.BlockSpec((tm, SB_WIDTH), row), pl.BlockSpec((tm, SWA_WIDTH), row),
                  pl.BlockSpec((tm, D_MODEL), row), _layer_spec((1, SB_WIDTH), layer),
                  _layer_spec((1, SWA_WIDTH), layer), _layer_spec((MIX_WIDTH, D_MODEL), layer),
                  _layer_spec((1, D_MODEL), layer), _layer_spec((D_MODEL, 2 * D_FF), layer),
                  _layer_spec((D_FF, D_MODEL), layer), _const_spec((1, D_MODEL))],
        out_specs=pl.BlockSpec((tm, D_MODEL), row),
        scratch_shapes=[pltpu.VMEM((tm, D_MODEL), BF16), pltpu.VMEM((tm, D_FF), BF16)],
        compiler_params=_params("parallel"),
        name="out_ffn",
    )(o_sb, o_sw, h, gsb, gsw, wout, g2, wgu, wd, gf)


def _sb_attention(proj, u2, batch, seq):
    T = SB_TILE
    pairs = SB_WIDTH // LANES
    diag_slots, off_slots = _sb_schedule(seq // T)
    table = jnp.asarray(np.array(diag_slots + off_slots, dtype=np.int32).reshape(-1))
    init_tiles = tuple(s[2] for s in diag_slots if s[4])
    col_block = lambda c: (lambda b, p, tbl: (b, c * pairs + p))
    return pl.pallas_call(
        functools.partial(_sb_kernel, len(diag_slots), len(off_slots), init_tiles),
        out_shape=jax.ShapeDtypeStruct((batch * seq, SB_WIDTH), F32),
        grid_spec=pltpu.PrefetchScalarGridSpec(
            num_scalar_prefetch=1,
            grid=(batch, pairs),
            in_specs=[pl.BlockSpec((seq, LANES), col_block(0)), pl.BlockSpec((seq, LANES), col_block(1)),
                      pl.BlockSpec((seq, LANES), col_block(2)),
                      pl.BlockSpec((T, T), lambda b, p, tbl: (0, 0), pipeline_mode=pl.Buffered(1))],
            out_specs=pl.BlockSpec((seq, LANES), col_block(0)),
            scratch_shapes=[pltpu.VMEM((2, seq, LANES), F32), pltpu.VMEM((2, seq, LANES), F32)]),
        compiler_params=_params("parallel", "parallel"),
        name="sb_attn",
    )(table, proj, proj, proj, u2)


def _swa_attention(layer, proj, sinks, bias, batch, seq):
    q_col = 3 * SB_WIDTH // SWA_WIDTH
    k_col = (3 * SB_WIDTH + SWA_WIDTH) // KV_WIDTH
    col_block = lambda col: (lambda b: (b, col))
    return pl.pallas_call(
        functools.partial(_swa_kernel, layer),
        out_shape=jax.ShapeDtypeStruct((batch * seq, SWA_WIDTH), F32),
        grid=(batch,),
        in_specs=[pl.BlockSpec(memory_space=pltpu.SMEM),
                  pl.BlockSpec((seq, SWA_WIDTH), col_block(q_col)),
                  pl.BlockSpec((seq, KV_WIDTH), col_block(k_col)), pl.BlockSpec((seq, KV_WIDTH), col_block(k_col + 1)),
                  _const_spec((2, SWA_Q_HEADS, BLOCK, BLOCK))],
        out_specs=pl.BlockSpec((seq, SWA_WIDTH), col_block(0)),
        compiler_params=_params("parallel"),
        name="swa_attn",
    )(sinks, proj, proj, proj, bias)


def _pair_swa_heads(x, axis):
    shape = x.shape
    k = shape[axis] // SWA_Q_HEADS
    x = x.reshape(shape[:axis] + (SWA_KV_HEADS, SWA_GROUP, k) + shape[axis + 1:])
    return jnp.swapaxes(x, axis, axis + 1).reshape(shape)


def _swa_folded_bias(rel_bias):
    table = _swa_bias_table(rel_bias)
    a = np.arange(BLOCK)[:, None]
    c = np.arange(BLOCK)[None, :]
    upper = jnp.asarray(c > a)
    normal = jnp.where(upper, table[:, :, :BLOCK], table[:, :, BLOCK:])
    first = jnp.where(upper, NEG_INF, table[:, :, BLOCK:])
    return _pair_swa_heads(jnp.stack([normal, first]), 1)


def kernel(x, norm_ffn1, w_ffn1_gu, w_ffn1_down, norm_mix, w_in, sinks, norm_out_sb, norm_out_swa, w_out,
           norm_ffn2, w_ffn2_gu, w_ffn2_down, rel_bias, norm_final):
    batch, seq, d_model = x.shape
    depth = w_in.shape[0]
    assert d_model == D_MODEL and seq % SB_TILE == 0 and (batch * seq) % (2 * TOKEN_TILE) == 0

    j = np.arange(SB_TILE)[:, None]
    s = np.arange(SB_TILE)[None, :]
    u2 = jnp.asarray(j >= s, dtype=BF16)
    bias = _swa_folded_bias(rel_bias)
    q_lo, q_hi = 3 * SB_WIDTH, 3 * SB_WIDTH + SWA_WIDTH

    gain = lambda g: g.astype(F32).reshape(depth, 1, -1)
    win = jnp.concatenate([w_in[:, :, :q_lo], _pair_swa_heads(w_in[:, :, q_lo:q_hi], 2), w_in[:, :, q_hi:]],
                          axis=2).astype(BF16)
    wout = jnp.concatenate([w_out[:, :SB_WIDTH], _pair_swa_heads(w_out[:, SB_WIDTH:], 1)], axis=1).astype(BF16)
    wgu1, wd1 = w_ffn1_gu.astype(BF16), w_ffn1_down.astype(BF16)
    wgu2, wd2 = w_ffn2_gu.astype(BF16), w_ffn2_down.astype(BF16)
    g1, gm, g2, gsb = gain(norm_ffn1), gain(norm_mix), gain(norm_ffn2), gain(norm_out_sb)
    gsw = gain(_pair_swa_heads(norm_out_swa, 1))
    gf = norm_final.astype(F32).reshape(1, -1)
    sinks_paired = _pair_swa_heads(sinks.astype(F32), 1)

    h = x.reshape(batch * seq, d_model)
    for l in range(depth):
        h, proj = _ffn_in(l, h, g1, wgu1, wd1, gm, win)
        o_sb = _sb_attention(proj, u2, batch, seq)
        o_sw = _swa_attention(l, proj, sinks_paired, bias, batch, seq)
        h = _out_ffn(l, o_sb, o_sw, h, gsb, gsw, wout, g2, wgu2, wd2, gf, final_norm=(l == depth - 1))
    return h.reshape(batch, seq, d_model)
```

```python
import functools
import math

import jax
import jax.numpy as jnp
import numpy as np
from jax import lax
from jax.experimental import pallas as pl
from jax.experimental.pallas import tpu as pltpu

D_MODEL = 1024
HEAD_DIM = 64
SB_HEADS = 8
SWA_Q_HEADS = 8
SWA_KV_HEADS = 2
SWA_GROUP = SWA_Q_HEADS // SWA_KV_HEADS
WINDOW = 128
BLOCK = 128
N_BUCKETS = 32
MAX_DISTANCE = 128
D_FF = 2816
EPS = 1e-6
NEG_INF = -1e30

SB_WIDTH = SB_HEADS * HEAD_DIM
SWA_WIDTH = SWA_Q_HEADS * HEAD_DIM
KV_WIDTH = SWA_KV_HEADS * HEAD_DIM
MIX_WIDTH = SB_WIDTH + SWA_WIDTH
IN_WIDTH = 3 * SB_WIDTH + SWA_WIDTH + 2 * KV_WIDTH
Q_SCALE = HEAD_DIM ** -0.5
LOG2E = math.log2(math.e)

LANES = 128
V7X_VMEM_BYTES = 64 * 1024 * 1024
VMEM_LIMIT_BYTES = 56 * 1024 * 1024

TOKEN_TILE = 256
FF_CHUNK = 256
PROJ_CHUNK = 256
SB_TILE = 256
SLOT_FIELDS = 5
SLOT_GROUP = 6
SWA_BLOCK_GROUP = 8

F32 = jnp.float32
BF16 = jnp.bfloat16


def _rms_norm(x, g):
    return x * lax.rsqrt(jnp.mean(x * x, axis=-1, keepdims=True) + EPS) * g


def _const_spec(shape):
    return pl.BlockSpec(shape, lambda *_: (0,) * len(shape), pipeline_mode=pl.Buffered(1))


def _layer_spec(shape, layer):
    return pl.BlockSpec((None,) + shape, lambda *_: (layer,) + (0,) * len(shape), pipeline_mode=pl.Buffered(1))


def _norm_rows(n_scr, rows, x, g_ref):
    n_scr[rows, :] = _rms_norm(x, g_ref[...]).astype(BF16)


def _gate_up_rows(rows, wgu_ref, n_scr, a_scr):
    for c in range(D_FF // FF_CHUNK):
        lo = c * FF_CHUNK
        n = n_scr[rows, :]
        gate = jnp.dot(n, wgu_ref[:, lo:lo + FF_CHUNK], preferred_element_type=F32)
        up = jnp.dot(n, wgu_ref[:, D_FF + lo:D_FF + lo + FF_CHUNK], preferred_element_type=F32)
        act = gate / (1.0 + jnp.exp(-gate)) * up
        a_scr[rows, lo:lo + FF_CHUNK] = act.astype(BF16)


def _down_rows(rows, h, wd_ref, a_scr):
    return h + 0.5 * jnp.dot(a_scr[rows, :], wd_ref[...], preferred_element_type=F32)


def _project_rows(rows, win_ref, n_scr, proj_ref):
    for c in range(IN_WIDTH // PROJ_CHUNK):
        lo = c * PROJ_CHUNK
        p = jnp.dot(n_scr[rows, :], win_ref[:, lo:lo + PROJ_CHUNK], preferred_element_type=F32)
        if lo < SB_WIDTH:
            p = p * (Q_SCALE * LOG2E)
        elif 3 * SB_WIDTH <= lo < 3 * SB_WIDTH + SWA_WIDTH:
            p = p * Q_SCALE
        proj_ref[rows, lo:lo + PROJ_CHUNK] = p.astype(BF16)


HALF_A = slice(0, TOKEN_TILE)
HALF_B = slice(TOKEN_TILE, 2 * TOKEN_TILE)


def _ffn_in_kernel(h_ref, g1_ref, wgu_ref, wd_ref, gm_ref, win_ref, h_out_ref, proj_ref, n_scr, a_scr):
    _norm_rows(n_scr, HALF_A, h_ref[HALF_A, :], g1_ref)
    _gate_up_rows(HALF_A, wgu_ref, n_scr, a_scr)
    _norm_rows(n_scr, HALF_B, h_ref[HALF_B, :], g1_ref)
    h1a = _down_rows(HALF_A, h_ref[HALF_A, :], wd_ref, a_scr)
    h_out_ref[HALF_A, :] = h1a
    _gate_up_rows(HALF_B, wgu_ref, n_scr, a_scr)
    _norm_rows(n_scr, HALF_A, h_out_ref[HALF_A, :], gm_ref)
    h1b = _down_rows(HALF_B, h_ref[HALF_B, :], wd_ref, a_scr)
    h_out_ref[HALF_B, :] = h1b
    _project_rows(HALF_A, win_ref, n_scr, proj_ref)
    _norm_rows(n_scr, HALF_B, h_out_ref[HALF_B, :], gm_ref)
    _project_rows(HALF_B, win_ref, n_scr, proj_ref)


def _out_ffn_kernel(final_norm, osb_ref, osw_ref, h_ref, gsb_ref, gsw_ref, wout_ref, g2_ref, wgu_ref, wd_ref,
                    gf_ref, out_ref, n_scr, a_scr):
    def mix(rows):
        sb = _rms_norm(osb_ref[rows, :], gsb_ref[...]).astype(BF16)
        sw = _rms_norm(osw_ref[rows, :], gsw_ref[...]).astype(BF16)
        out_ref[rows, :] = (h_ref[rows, :]
                            + jnp.dot(sb, wout_ref[:SB_WIDTH, :], preferred_element_type=F32)
                            + jnp.dot(sw, wout_ref[SB_WIDTH:, :], preferred_element_type=F32))
        _norm_rows(n_scr, rows, out_ref[rows, :], g2_ref)

    def finish(rows):
        h3 = _down_rows(rows, out_ref[rows, :], wd_ref, a_scr)
        out_ref[rows, :] = _rms_norm(h3, gf_ref[...]) if final_norm else h3

    mix(HALF_A)
    _gate_up_rows(HALF_A, wgu_ref, n_scr, a_scr)
    mix(HALF_B)
    finish(HALF_A)
    _gate_up_rows(HALF_B, wgu_ref, n_scr, a_scr)
    finish(HALF_B)


def _sb_schedule(nq):
    nxt = list(range(nq))
    remaining = lambda a: nxt[a] + 1
    diag_slots, off_slots = [], []
    diag_left = list(range(nq - 1, -1, -1))
    while diag_left:
        a = diag_left.pop(0)
        nxt[a] -= 1
        ready = [b for b in range(nq) if b != a and 0 <= nxt[b] < b]
        if ready:
            b = max(ready, key=remaining)
            diag_slots.append((a, a, b, nxt[b], 0))
        else:
            b = diag_left.pop(0)
            diag_slots.append((a, a, b, b, 1))
        nxt[b] -= 1
    while any(n >= 0 for n in nxt):
        b1, b2 = sorted((b for b in range(nq) if nxt[b] >= 0), key=remaining, reverse=True)[:2]
        off_slots.append((b1, nxt[b1], b2, nxt[b2], 0))
        nxt[b1] -= 1
        nxt[b2] -= 1
    return diag_slots, off_slots


def _sb_kernel(n_diag_slots, n_off_slots, init_tiles, tbl_ref, q_ref, k_ref, v_ref, u_ref, o_ref, acc_ref, r_ref):
    T = SB_TILE
    H = T // 2
    iota = lambda shape, axis: lax.broadcasted_iota(jnp.int32, shape, axis)
    causal = iota((T, T), 1) < iota((T, T), 0)
    causal_upper = iota((H, H), 1) < iota((H, H), 0)
    causal_lower = iota((H, T), 1) < iota((H, T), 0) + H
    head_lanes = {n: [iota((n, LANES), 1) < HEAD_DIM, iota((n, LANES), 1) >= HEAD_DIM] for n in (H, T)}

    for qt in init_tiles:
        acc_ref[:, qt * T:(qt + 1) * T, :] = jnp.zeros((2, T, LANES), F32)
        r_ref[:, qt * T:(qt + 1) * T, :] = jnp.zeros((2, T, LANES), F32)

    class Slot:
        def __init__(self, index, masked):
            base = index * SLOT_FIELDS
            self.items = []
            for n in range(2):
                qs = pl.multiple_of(tbl_ref[base + 2 * n] * T, T)
                ks = pl.multiple_of(tbl_ref[base + 2 * n + 1] * T, T)
                for hh in range(2):
                    if masked and n == 0:
                        self.items.append((hh, pl.ds(qs, H), pl.ds(ks, H), causal_upper, True))
                        self.items.append((hh, pl.ds(pl.multiple_of(qs + H, H), H), pl.ds(ks, T), causal_lower, True))
                    else:
                        mask = causal | (tbl_ref[base + 4] == 0) if masked else None
                        self.items.append((hh, pl.ds(qs, T), pl.ds(ks, T), mask, False))

        def scores(self):
            self.z = []
            for hh, q_rows, k_rows, mask, _ in self.items:
                q = q_ref[q_rows, :]
                qh = jnp.where(head_lanes[q.shape[0]][hh], q, jnp.zeros_like(q))
                z = lax.dot_general(qh, k_ref[k_rows, :], (((1,), (1,)), ((), ())), preferred_element_type=F32)
                if mask is not None:
                    z = jnp.where(mask, z, NEG_INF)
                self.z.append(z)

        def softplus(self):
            self.lhs = []
            for z in self.z:
                zb = z.astype(BF16)
                neg_abs = lax.bitcast_convert_type(
                    lax.bitcast_convert_type(zb, jnp.uint16) | jnp.uint16(0x8000), BF16)
                log2_1p = jnp.log(1.0 + jnp.exp2(neg_abs)) * jnp.asarray(LOG2E, BF16)
                self.lhs.append(jnp.maximum(zb, 0.0) + log2_1p)

        def suffix_sums(self):
            self.incl = [jnp.dot(x, u_ref[:x.shape[1], :x.shape[1]], preferred_element_type=F32) for x in self.lhs]

        def weights(self):
            self.w = []
            for (hh, q_rows, _, _, first), z, incl in zip(self.items, self.z, self.incl):
                row_sum = jnp.broadcast_to(incl[:, 0:1], (z.shape[0], LANES))
                if first:
                    self.w.append(jnp.exp2(z - incl).astype(BF16))
                    r_ref[hh, q_rows, :] = row_sum
                else:
                    r = r_ref[hh, q_rows, :]
                    self.w.append(jnp.exp2(z - incl - jnp.concatenate([r] * (z.shape[1] // LANES), axis=1)).astype(BF16))
                    r_ref[hh, q_rows, :] = r + row_sum

        def values(self):
            self.pv = [jnp.dot(w, v_ref[k_rows, :], preferred_element_type=F32)
                       for (_, _, k_rows, _, _), w in zip(self.items, self.w)]

        def accumulate(self):
            for (hh, q_rows, _, _, first), pv in zip(self.items, self.pv):
                if first:
                    acc_ref[hh, q_rows, :] = pv
                else:
                    acc_ref[hh, q_rows, :] += pv

    def staggered(slots):
        for prev, cur in zip([None] + slots, slots + [None]):
            if cur is not None:
                cur.scores()
            if prev is not None:
                prev.weights()
                prev.values()
            if cur is not None:
                cur.softplus()
            if prev is not None:
                prev.accumulate()
            if cur is not None:
                cur.suffix_sums()

    def group_loop(first, n_groups, masked):
        def body(i, _):
            staggered([Slot(first + SLOT_GROUP * i + n, masked) for n in range(SLOT_GROUP)])
            return 0
        lax.fori_loop(0, n_groups, body, 0)

    n_diag_groups = n_diag_slots // SLOT_GROUP
    n_mixed_diag = n_diag_slots % SLOT_GROUP
    n_mixed_off = (SLOT_GROUP - n_mixed_diag) % SLOT_GROUP
    assert (n_off_slots - n_mixed_off) % SLOT_GROUP == 0
    group_loop(0, n_diag_groups, True)
    if n_mixed_diag:
        staggered([Slot(n_diag_groups * SLOT_GROUP + n, True) for n in range(n_mixed_diag)]
                  + [Slot(n_diag_slots + n, False) for n in range(n_mixed_off)])
    group_loop(n_diag_slots + n_mixed_off, (n_off_slots - n_mixed_off) // SLOT_GROUP, False)
    o_ref[...] = jnp.where(lax.broadcasted_iota(jnp.int32, o_ref.shape, 1) < HEAD_DIM, acc_ref[0], acc_ref[1])


def _swa_kernel(layer, sink_ref, q_ref, k_ref, v_ref, bias_ref, o_ref):
    n_blocks = q_ref.shape[0] // BLOCK
    n_slots = SWA_Q_HEADS // 2
    lane = lax.broadcasted_iota(jnp.int32, (BLOCK, LANES), 1)
    a = lax.broadcasted_iota(jnp.int32, (BLOCK, BLOCK), 0)
    c = lax.broadcasted_iota(jnp.int32, (BLOCK, BLOCK), 1)
    upper = c > a
    half = [lane < HEAD_DIM, lane >= HEAD_DIM]

    heads = [(p, sub) for p in range(n_slots) for sub in range(2)]

    class Block:
        def __init__(self, i):
            self.cur = pl.multiple_of(i * BLOCK, BLOCK)
            self.prev = pl.multiple_of(jnp.maximum(i - 1, 0) * BLOCK, BLOCK)
            self.table = 1 - jnp.minimum(i, 1)

        def rows(self, ref):
            return jnp.concatenate([ref[pl.ds(self.prev, BLOCK), :], ref[pl.ds(self.cur, BLOCK), :]], axis=0)

        def scores(self):
            kcat = self.rows(k_ref)
            self.s = []
            for p, sub in heads:
                q = q_ref[pl.ds(self.cur, BLOCK), p * LANES:(p + 1) * LANES]
                s = lax.dot_general(jnp.where(half[sub], q, jnp.zeros_like(q)), kcat, (((1,), (1,)), ((), ())),
                                    preferred_element_type=F32)
                self.s.append(jnp.where(upper, s[:, :BLOCK], s[:, BLOCK:]) + bias_ref[self.table, 2 * p + sub])

        def softmax(self):
            self.p = []
            for (p, sub), s in zip(heads, self.s):
                sink = sink_ref[layer, 2 * p + sub]
                m = jnp.maximum(jnp.max(s, axis=-1, keepdims=True), sink)
                e = jnp.exp(s - m)
                e = (e * (1.0 / (jnp.sum(e, axis=-1, keepdims=True) + jnp.exp(sink - m)))).astype(BF16)
                zero = jnp.zeros_like(e)
                self.p.append(jnp.concatenate([jnp.where(upper, e, zero), jnp.where(upper, zero, e)], axis=1))

        def values(self):
            vcat = self.rows(v_ref)
            self.o = [jnp.dot(pr, vcat, preferred_element_type=F32) for pr in self.p]

        def store(self):
            for p in range(n_slots):
                o_ref[pl.ds(self.cur, BLOCK), p * LANES:(p + 1) * LANES] = jnp.where(
                    half[0], self.o[2 * p], self.o[2 * p + 1])

    def block_group(i, _):
        blocks = [Block(SWA_BLOCK_GROUP * i + n) for n in range(SWA_BLOCK_GROUP)]
        for prev, cur in zip([None] + blocks, blocks + [None]):
            if cur is not None:
                cur.scores()
            if prev is not None:
                prev.values()
            if cur is not None:
                cur.softmax()
            if prev is not None:
                prev.store()
        return 0

    assert n_blocks % SWA_BLOCK_GROUP == 0
    lax.fori_loop(0, n_blocks // SWA_BLOCK_GROUP, block_group, 0)


def _t5_causal_bucket(dist):
    max_exact = N_BUCKETS // 2
    d = jnp.maximum(dist, 1).astype(F32)
    large = max_exact + (jnp.log(d / max_exact) / math.log(MAX_DISTANCE / max_exact)
                         * (N_BUCKETS - max_exact)).astype(jnp.int32)
    large = jnp.minimum(large, N_BUCKETS - 1)
    return jnp.where(dist < max_exact, dist, large)


def _swa_bias_table(rel_bias):
    period = 3 * BLOCK - 1
    dist = jnp.arange(period) - (BLOCK - 1)
    g = rel_bias.astype(F32)[_t5_causal_bucket(jnp.maximum(dist, 0))].T
    flat = jnp.tile(g, (1, BLOCK + 1))[:, :BLOCK * (period + 1)]
    return flat.reshape(SWA_Q_HEADS, BLOCK, period + 1)[:, :, :2 * BLOCK][:, :, ::-1]


def _params(*semantics):
    return pltpu.CompilerParams(dimension_semantics=semantics, vmem_limit_bytes=VMEM_LIMIT_BYTES)


def _ffn_in(layer, h, g1, wgu, wd, gm, win):
    n_tok = h.shape[0]
    tm = 2 * TOKEN_TILE
    row = lambda i: (i, 0)
    return pl.pallas_call(
        _ffn_in_kernel,
        out_shape=(jax.ShapeDtypeStruct((n_tok, D_MODEL), F32), jax.ShapeDtypeStruct((n_tok, IN_WIDTH), BF16)),
        grid=(n_tok // tm,),
        in_specs=[pl.BlockSpec((tm, D_MODEL), row), _layer_spec((1, D_MODEL), layer),
                  _layer_spec((D_MODEL, 2 * D_FF), layer), _layer_spec((D_FF, D_MODEL), layer),
                  _layer_spec((1, D_MODEL), layer), _layer_spec((D_MODEL, IN_WIDTH), layer)],
        out_specs=(pl.BlockSpec((tm, D_MODEL), row), pl.BlockSpec((tm, IN_WIDTH), row)),
        scratch_shapes=[pltpu.VMEM((tm, D_MODEL), BF16), pltpu.VMEM((tm, D_FF), BF16)],
        compiler_params=_params("parallel"),
        name="ffn_in",
    )(h, g1, wgu, wd, gm, win)


def _out_ffn(layer, o_sb, o_sw, h, gsb, gsw, wout, g2, wgu, wd, gf, final_norm):
    n_tok = h.shape[0]
    tm = 2 * TOKEN_TILE
    row = lambda i: (i, 0)
    return pl.pallas_call(
        functools.partial(_out_ffn_kernel, final_norm),
        out_shape=jax.ShapeDtypeStruct((n_tok, D_MODEL), F32),
        grid=(n_tok // tm,),
        in_specs=[pl.BlockSpec((tm, SB_WIDTH), row), pl.BlockSpec((tm, SWA_WIDTH), row),
                  pl.BlockSpec((tm, D_MODEL), row), _layer_spec((1, SB_WIDTH), layer),
                  _layer_spec((1, SWA_WIDTH), layer), _layer_spec((MIX_WIDTH, D_MODEL), layer),
                  _layer_spec((1, D_MODEL), layer), _layer_spec((D_MODEL, 2 * D_FF), layer),
                  _layer_spec((D_FF, D_MODEL), layer), _const_spec((1, D_MODEL))],
        out_specs=pl.BlockSpec((tm, D_MODEL), row),
        scratch_shapes=[pltpu.VMEM((tm, D_MODEL), BF16), pltpu.VMEM((tm, D_FF), BF16)],
        compiler_params=_params("parallel"),
        name="out_ffn",
    )(o_sb, o_sw, h, gsb, gsw, wout, g2, wgu, wd, gf)


def _sb_attention(proj, u2, batch, seq):
    T = SB_TILE
    pairs = SB_WIDTH // LANES
    diag_slots, off_slots = _sb_schedule(seq // T)
    table = jnp.asarray(np.array(diag_slots + off_slots, dtype=np.int32).reshape(-1))
    init_tiles = tuple(s[2] for s in diag_slots if s[4])
    col_block = lambda c: (lambda b, p, tbl: (b, c * pairs + p))
    return pl.pallas_call(
        functools.partial(_sb_kernel, len(diag_slots), len(off_slots), init_tiles),
        out_shape=jax.ShapeDtypeStruct((batch * seq, SB_WIDTH), F32),
        grid_spec=pltpu.PrefetchScalarGridSpec(
            num_scalar_prefetch=1,
            grid=(batch, pairs),
            in_specs=[pl.BlockSpec((seq, LANES), col_block(0)), pl.BlockSpec((seq, LANES), col_block(1)),
                      pl.BlockSpec((seq, LANES), col_block(2)),
                      pl.BlockSpec((T, T), lambda b, p, tbl: (0, 0), pipeline_mode=pl.Buffered(1))],
            out_specs=pl.BlockSpec((seq, LANES), col_block(0)),
            scratch_shapes=[pltpu.VMEM((2, seq, LANES), F32), pltpu.VMEM((2, seq, LANES), F32)]),
        compiler_params=_params("parallel", "parallel"),
        name="sb_attn",
    )(table, proj, proj, proj, u2)


def _swa_attention(layer, proj, sinks, bias, batch, seq):
    q_col = 3 * SB_WIDTH // SWA_WIDTH
    k_col = (3 * SB_WIDTH + SWA_WIDTH) // KV_WIDTH
    col_block = lambda col: (lambda b: (b, col))
    return pl.pallas_call(
        functools.partial(_swa_kernel, layer),
        out_shape=jax.ShapeDtypeStruct((batch * seq, SWA_WIDTH), F32),
        grid=(batch,),
        in_specs=[pl.BlockSpec(memory_space=pltpu.SMEM),
                  pl.BlockSpec((seq, SWA_WIDTH), col_block(q_col)),
                  pl.BlockSpec((seq, KV_WIDTH), col_block(k_col)), pl.BlockSpec((seq, KV_WIDTH), col_block(k_col + 1)),
                  _const_spec((2, SWA_Q_HEADS, BLOCK, BLOCK))],
        out_specs=pl.BlockSpec((seq, SWA_WIDTH), col_block(0)),
        compiler_params=_params("parallel"),
        name="swa_attn",
    )(sinks, proj, proj, proj, bias)


def _pair_swa_heads(x, axis):
    shape = x.shape
    k = shape[axis] // SWA_Q_HEADS
    x = x.reshape(shape[:axis] + (SWA_KV_HEADS, SWA_GROUP, k) + shape[axis + 1:])
    return jnp.swapaxes(x, axis, axis + 1).reshape(shape)


def _swa_folded_bias(rel_bias):
    table = _swa_bias_table(rel_bias)
    a = np.arange(BLOCK)[:, None]
    c = np.arange(BLOCK)[None, :]
    upper = jnp.asarray(c > a)
    normal = jnp.where(upper, table[:, :, :BLOCK], table[:, :, BLOCK:])
    first = jnp.where(upper, NEG_INF, table[:, :, BLOCK:])
    return _pair_swa_heads(jnp.stack([normal, first]), 1)


def kernel(x, norm_ffn1, w_ffn1_gu, w_ffn1_down, norm_mix, w_in, sinks, norm_out_sb, norm_out_swa, w_out,
           norm_ffn2, w_ffn2_gu, w_ffn2_down, rel_bias, norm_final):
    batch, seq, d_model = x.shape
    depth = w_in.shape[0]
    assert d_model == D_MODEL and seq % SB_TILE == 0 and (batch * seq) % (2 * TOKEN_TILE) == 0

    j = np.arange(SB_TILE)[:, None]
    s = np.arange(SB_TILE)[None, :]
    u2 = jnp.asarray(j >= s, dtype=BF16)
    bias = _swa_folded_bias(rel_bias)
    q_lo, q_hi = 3 * SB_WIDTH, 3 * SB_WIDTH + SWA_WIDTH

    gain = lambda g: g.astype(F32).reshape(depth, 1, -1)
    win = jnp.concatenate([w_in[:, :, :q_lo], _pair_swa_heads(w_in[:, :, q_lo:q_hi], 2), w_in[:, :, q_hi:]],
                          axis=2).astype(BF16)
    wout = jnp.concatenate([w_out[:, :SB_WIDTH], _pair_swa_heads(w_out[:, SB_WIDTH:], 1)], axis=1).astype(BF16)
    wgu1, wd1 = w_ffn1_gu.astype(BF16), w_ffn1_down.astype(BF16)
    wgu2, wd2 = w_ffn2_gu.astype(BF16), w_ffn2_down.astype(BF16)
    g1, gm, g2, gsb = gain(norm_ffn1), gain(norm_mix), gain(norm_ffn2), gain(norm_out_sb)
    gsw = gain(_pair_swa_heads(norm_out_swa, 1))
    gf = norm_final.astype(F32).reshape(1, -1)
    sinks_paired = _pair_swa_heads(sinks.astype(F32), 1)

    h = x.reshape(batch * seq, d_model)
    for l in range(depth):
        h, proj = _ffn_in(l, h, g1, wgu1, wd1, gm, win)
        o_sb = _sb_attention(proj, u2, batch, seq)
        o_sw = _swa_attention(l, proj, sinks_paired, bias, batch, seq)
        h = _out_ffn(l, o_sb, o_sw, h, gsb, gsw, wout, g2, wgu2, wd2, gf, final_norm=(l == depth - 1))
    return h.reshape(batch, seq, d_model)
```

```python
import functools
import math

import jax
import jax.numpy as jnp
import numpy as np
from jax import lax
from jax.experimental import pallas as pl
from jax.experimental.pallas import tpu as pltpu

D_MODEL = 1024
HEAD_DIM = 64
SB_HEADS = 8
SWA_Q_HEADS = 8
SWA_KV_HEADS = 2
SWA_GROUP = SWA_Q_HEADS // SWA_KV_HEADS
WINDOW = 128
BLOCK = 128
N_BUCKETS = 32
MAX_DISTANCE = 128
D_FF = 2816
EPS = 1e-6
NEG_INF = -1e30

SB_WIDTH = SB_HEADS * HEAD_DIM
SWA_WIDTH = SWA_Q_HEADS * HEAD_DIM
KV_WIDTH = SWA_KV_HEADS * HEAD_DIM
MIX_WIDTH = SB_WIDTH + SWA_WIDTH
IN_WIDTH = 3 * SB_WIDTH + SWA_WIDTH + 2 * KV_WIDTH
Q_SCALE = HEAD_DIM ** -0.5
LOG2E = math.log2(math.e)

LANES = 128
V7X_VMEM_BYTES = 64 * 1024 * 1024
VMEM_LIMIT_BYTES = V7X_VMEM_BYTES * 7 // 8
BF16_SIGN_BIT = 0x8000

TOKEN_TILE = 256
OUT_FFN_ROW_GROUPS = 4
FF_CHUNK = 256
PROJ_CHUNK = 256
SB_TILE = 256
SLOT_FIELDS = 5
SLOT_GROUP = 6
SWA_BLOCK_GROUP = 8

F32 = jnp.float32
BF16 = jnp.bfloat16


def _rms_norm(x, g):
    return x * lax.rsqrt(jnp.mean(x * x, axis=-1, keepdims=True) + EPS) * g


def _const_spec(shape):
    return pl.BlockSpec(shape, lambda *_: (0,) * len(shape), pipeline_mode=pl.Buffered(1))


def _layer_spec(shape, layer):
    return pl.BlockSpec((None,) + shape, lambda *_: (layer,) + (0,) * len(shape), pipeline_mode=pl.Buffered(1))


def _norm_rows(n_scr, rows, x, g_ref):
    n_scr[rows, :] = _rms_norm(x, g_ref[...]).astype(BF16)


def _gate_up_rows(rows, wgu_ref, n_scr, a_scr):
    for c in range(D_FF // FF_CHUNK):
        lo = c * FF_CHUNK
        n = n_scr[rows, :]
        gate = jnp.dot(n, wgu_ref[:, lo:lo + FF_CHUNK], preferred_element_type=F32)
        up = jnp.dot(n, wgu_ref[:, D_FF + lo:D_FF + lo + FF_CHUNK], preferred_element_type=F32)
        act = gate / (1.0 + jnp.exp(-gate)) * up
        a_scr[rows, lo:lo + FF_CHUNK] = act.astype(BF16)


def _down_rows(rows, h, wd_ref, a_scr):
    return h + 0.5 * jnp.dot(a_scr[rows, :], wd_ref[...], preferred_element_type=F32)


def _project_rows(rows, win_ref, n_scr, proj_ref):
    for c in range(IN_WIDTH // PROJ_CHUNK):
        lo = c * PROJ_CHUNK
        p = jnp.dot(n_scr[rows, :], win_ref[:, lo:lo + PROJ_CHUNK], preferred_element_type=F32)
        if lo < SB_WIDTH:
            p = p * (Q_SCALE * LOG2E)
        elif 3 * SB_WIDTH <= lo < 3 * SB_WIDTH + SWA_WIDTH:
            p = p * Q_SCALE
        proj_ref[rows, lo:lo + PROJ_CHUNK] = p.astype(BF16)


HALF_A = slice(0, TOKEN_TILE)
HALF_B = slice(TOKEN_TILE, 2 * TOKEN_TILE)


def _ffn_in_kernel(h_ref, g1_ref, wgu_ref, wd_ref, gm_ref, win_ref, h_out_ref, proj_ref, n_scr, a_scr):
    _norm_rows(n_scr, HALF_A, h_ref[HALF_A, :], g1_ref)
    _gate_up_rows(HALF_A, wgu_ref, n_scr, a_scr)
    _norm_rows(n_scr, HALF_B, h_ref[HALF_B, :], g1_ref)
    h1a = _down_rows(HALF_A, h_ref[HALF_A, :], wd_ref, a_scr)
    h_out_ref[HALF_A, :] = h1a
    _gate_up_rows(HALF_B, wgu_ref, n_scr, a_scr)
    _norm_rows(n_scr, HALF_A, h_out_ref[HALF_A, :], gm_ref)
    h1b = _down_rows(HALF_B, h_ref[HALF_B, :], wd_ref, a_scr)
    h_out_ref[HALF_B, :] = h1b
    _project_rows(HALF_A, win_ref, n_scr, proj_ref)
    _norm_rows(n_scr, HALF_B, h_out_ref[HALF_B, :], gm_ref)
    _project_rows(HALF_B, win_ref, n_scr, proj_ref)


def _out_ffn_kernel(final_norm, osb_ref, osw_ref, h_ref, gsb_ref, gsw_ref, wout_ref, g2_ref, wgu_ref, wd_ref,
                    gf_ref, out_ref, n_scr, a_scr):
    def mix(rows):
        sb = _rms_norm(osb_ref[rows, :], gsb_ref[...]).astype(BF16)
        sw = _rms_norm(osw_ref[rows, :], gsw_ref[...]).astype(BF16)
        out_ref[rows, :] = (h_ref[rows, :]
                            + jnp.dot(sb, wout_ref[:SB_WIDTH, :], preferred_element_type=F32)
                            + jnp.dot(sw, wout_ref[SB_WIDTH:, :], preferred_element_type=F32))
        _norm_rows(n_scr, rows, out_ref[rows, :], g2_ref)

    def finish(rows):
        h3 = _down_rows(rows, out_ref[rows, :], wd_ref, a_scr)
        out_ref[rows, :] = _rms_norm(h3, gf_ref[...]) if final_norm else h3

    groups = [slice(r, r + TOKEN_TILE) for r in range(0, out_ref.shape[0], TOKEN_TILE)]
    for prev, cur in zip([None] + groups, groups + [None]):
        if cur is not None:
            mix(cur)
        if prev is not None:
            finish(prev)
        if cur is not None:
            _gate_up_rows(cur, wgu_ref, n_scr, a_scr)


def _sb_schedule(nq):
    nxt = list(range(nq))
    remaining = lambda a: nxt[a] + 1
    diag_slots, off_slots = [], []
    diag_left = list(range(nq - 1, -1, -1))
    while diag_left:
        a = diag_left.pop(0)
        nxt[a] -= 1
        ready = [b for b in range(nq) if b != a and 0 <= nxt[b] < b]
        if ready:
            b = max(ready, key=remaining)
            diag_slots.append((a, a, b, nxt[b], 0))
        else:
            b = diag_left.pop(0)
            diag_slots.append((a, a, b, b, 1))
        nxt[b] -= 1
    while any(n >= 0 for n in nxt):
        b1, b2 = sorted((b for b in range(nq) if nxt[b] >= 0), key=remaining, reverse=True)[:2]
        off_slots.append((b1, nxt[b1], b2, nxt[b2], 0))
        nxt[b1] -= 1
        nxt[b2] -= 1
    return diag_slots, off_slots


def _sb_kernel(n_diag_slots, n_off_slots, init_tiles, tbl_ref, q_ref, k_ref, v_ref, u_ref, o_ref, acc_ref, r_ref):
    T = SB_TILE
    H = T // 2
    iota = lambda shape, axis: lax.broadcasted_iota(jnp.int32, shape, axis)
    causal = iota((T, T), 1) < iota((T, T), 0)
    causal_upper = iota((H, H), 1) < iota((H, H), 0)
    causal_lower = iota((H, T), 1) < iota((H, T), 0) + H
    head_lanes = {n: [iota((n, LANES), 1) < HEAD_DIM, iota((n, LANES), 1) >= HEAD_DIM] for n in (H, T)}

    for qt in init_tiles:
        acc_ref[:, qt * T:(qt + 1) * T, :] = jnp.zeros((2, T, LANES), F32)
        r_ref[:, qt * T:(qt + 1) * T, :] = jnp.zeros((2, T, LANES), F32)

    class Slot:
        def __init__(self, index, masked):
            base = index * SLOT_FIELDS
            self.items = []
            for n in range(2):
                qs = pl.multiple_of(tbl_ref[base + 2 * n] * T, T)
                ks = pl.multiple_of(tbl_ref[base + 2 * n + 1] * T, T)
                for hh in range(2):
                    if masked and n == 0:
                        self.items.append((hh, pl.ds(qs, H), pl.ds(ks, H), causal_upper, True))
                        self.items.append((hh, pl.ds(pl.multiple_of(qs + H, H), H), pl.ds(ks, T), causal_lower, True))
                    else:
                        mask = causal | (tbl_ref[base + 4] == 0) if masked else None
                        self.items.append((hh, pl.ds(qs, T), pl.ds(ks, T), mask, False))

        def scores(self):
            self.z = []
            for hh, q_rows, k_rows, mask, _ in self.items:
                q = q_ref[q_rows, :]
                qh = jnp.where(head_lanes[q.shape[0]][hh], q, jnp.zeros_like(q))
                z = lax.dot_general(qh, k_ref[k_rows, :], (((1,), (1,)), ((), ())), preferred_element_type=F32)
                if mask is not None:
                    z = jnp.where(mask, z, NEG_INF)
                self.z.append(z)

        def softplus(self):
            self.lhs = []
            for z in self.z:
                zb = z.astype(BF16)
                neg_abs = lax.bitcast_convert_type(
                    lax.bitcast_convert_type(zb, jnp.uint16) | jnp.uint16(BF16_SIGN_BIT), BF16)
                log2_1p = jnp.log(1.0 + jnp.exp2(neg_abs)) * jnp.asarray(LOG2E, BF16)
                self.lhs.append(jnp.maximum(zb, 0.0) + log2_1p)

        def suffix_sums(self):
            self.incl = [jnp.dot(x, u_ref[:x.shape[1], :x.shape[1]], preferred_element_type=F32) for x in self.lhs]

        def weights(self):
            self.w = []
            for (hh, q_rows, _, _, first), z, incl in zip(self.items, self.z, self.incl):
                row_sum = jnp.broadcast_to(incl[:, 0:1], (z.shape[0], LANES))
                if first:
                    self.w.append(jnp.exp2(z - incl).astype(BF16))
                    r_ref[hh, q_rows, :] = row_sum
                else:
                    r = r_ref[hh, q_rows, :]
                    self.w.append(jnp.exp2(z - incl - jnp.concatenate([r] * (z.shape[1] // LANES), axis=1)).astype(BF16))
                    r_ref[hh, q_rows, :] = r + row_sum

        def values(self):
            self.pv = [jnp.dot(w, v_ref[k_rows, :], preferred_element_type=F32)
                       for (_, _, k_rows, _, _), w in zip(self.items, self.w)]

        def accumulate(self):
            for (hh, q_rows, _, _, first), pv in zip(self.items, self.pv):
                if first:
                    acc_ref[hh, q_rows, :] = pv
                else:
                    acc_ref[hh, q_rows, :] += pv

    def staggered(slots):
        for prev, cur in zip([None] + slots, slots + [None]):
            if cur is not None:
                cur.scores()
            if prev is not None:
                prev.weights()
                prev.values()
            if cur is not None:
                cur.softplus()
            if prev is not None:
                prev.accumulate()
            if cur is not None:
                cur.suffix_sums()

    def group_loop(first, n_groups, masked):
        def body(i, _):
            staggered([Slot(first + SLOT_GROUP * i + n, masked) for n in range(SLOT_GROUP)])
            return 0
        lax.fori_loop(0, n_groups, body, 0)

    n_diag_groups = n_diag_slots // SLOT_GROUP
    n_mixed_diag = n_diag_slots % SLOT_GROUP
    n_mixed_off = (SLOT_GROUP - n_mixed_diag) % SLOT_GROUP
    assert (n_off_slots - n_mixed_off) % SLOT_GROUP == 0
    group_loop(0, n_diag_groups, True)
    if n_mixed_diag:
        staggered([Slot(n_diag_groups * SLOT_GROUP + n, True) for n in range(n_mixed_diag)]
                  + [Slot(n_diag_slots + n, False) for n in range(n_mixed_off)])
    group_loop(n_diag_slots + n_mixed_off, (n_off_slots - n_mixed_off) // SLOT_GROUP, False)
    o_ref[...] = jnp.where(lax.broadcasted_iota(jnp.int32, o_ref.shape, 1) < HEAD_DIM, acc_ref[0], acc_ref[1])


def _swa_kernel(layer, sink_ref, q_ref, k_ref, v_ref, bias_ref, o_ref):
    n_blocks = q_ref.shape[0] // BLOCK
    n_slots = SWA_Q_HEADS // 2
    lane = lax.broadcasted_iota(jnp.int32, (BLOCK, LANES), 1)
    a = lax.broadcasted_iota(jnp.int32, (BLOCK, BLOCK), 0)
    c = lax.broadcasted_iota(jnp.int32, (BLOCK, BLOCK), 1)
    upper = c > a
    half = [lane < HEAD_DIM, lane >= HEAD_DIM]

    heads = [(p, sub) for p in range(n_slots) for sub in range(2)]

    class Block:
        def __init__(self, i):
            self.cur = pl.multiple_of(i * BLOCK, BLOCK)
            self.prev = pl.multiple_of(jnp.maximum(i - 1, 0) * BLOCK, BLOCK)
            self.table = 1 - jnp.minimum(i, 1)

        def rows(self, ref):
            return jnp.concatenate([ref[pl.ds(self.prev, BLOCK), :], ref[pl.ds(self.cur, BLOCK), :]], axis=0)

        def scores(self):
            kcat = self.rows(k_ref)
            self.s = []
            for p, sub in heads:
                q = q_ref[pl.ds(self.cur, BLOCK), p * LANES:(p + 1) * LANES]
                s = lax.dot_general(jnp.where(half[sub], q, jnp.zeros_like(q)), kcat, (((1,), (1,)), ((), ())),
                                    preferred_element_type=F32)
                self.s.append(jnp.where(upper, s[:, :BLOCK], s[:, BLOCK:]) + bias_ref[self.table, 2 * p + sub])

        def softmax(self):
            self.p = []
            for (p, sub), s in zip(heads, self.s):
                sink = sink_ref[layer, 2 * p + sub]
                m = jnp.maximum(jnp.max(s, axis=-1, keepdims=True), sink)
                e = jnp.exp(s - m)
                e = (e * (1.0 / (jnp.sum(e, axis=-1, keepdims=True) + jnp.exp(sink - m)))).astype(BF16)
                zero = jnp.zeros_like(e)
                self.p.append(jnp.concatenate([jnp.where(upper, e, zero), jnp.where(upper, zero, e)], axis=1))

        def values(self):
            vcat = self.rows(v_ref)
            self.o = [jnp.dot(pr, vcat, preferred_element_type=F32) for pr in self.p]

        def store(self):
            for p in range(n_slots):
                o_ref[pl.ds(self.cur, BLOCK), p * LANES:(p + 1) * LANES] = jnp.where(
                    half[0], self.o[2 * p], self.o[2 * p + 1])

    def block_group(i, _):
        blocks = [Block(SWA_BLOCK_GROUP * i + n) for n in range(SWA_BLOCK_GROUP)]
        for prev, cur in zip([None] + blocks, blocks + [None]):
            if cur is not None:
                cur.scores()
            if prev is not None:
                prev.values()
            if cur is not None:
                cur.softmax()
            if prev is not None:
                prev.store()
        return 0

    assert n_blocks % SWA_BLOCK_GROUP == 0
    lax.fori_loop(0, n_blocks // SWA_BLOCK_GROUP, block_group, 0)


def _t5_causal_bucket(dist):
    max_exact = N_BUCKETS // 2
    d = jnp.maximum(dist, 1).astype(F32)
    large = max_exact + (jnp.log(d / max_exact) / math.log(MAX_DISTANCE / max_exact)
                         * (N_BUCKETS - max_exact)).astype(jnp.int32)
    large = jnp.minimum(large, N_BUCKETS - 1)
    return jnp.where(dist < max_exact, dist, large)


def _swa_bias_table(rel_bias):
    period = 3 * BLOCK - 1
    dist = jnp.arange(period) - (BLOCK - 1)
    g = rel_bias.astype(F32)[_t5_causal_bucket(jnp.maximum(dist, 0))].T
    flat = jnp.tile(g, (1, BLOCK + 1))[:, :BLOCK * (period + 1)]
    return flat.reshape(SWA_Q_HEADS, BLOCK, period + 1)[:, :, :2 * BLOCK][:, :, ::-1]


def _params(*semantics):
    return pltpu.CompilerParams(dimension_semantics=semantics, vmem_limit_bytes=VMEM_LIMIT_BYTES)


def _ffn_in(layer, h, g1, wgu, wd, gm, win):
    n_tok = h.shape[0]
    tm = 2 * TOKEN_TILE
    row = lambda i: (i, 0)
    return pl.pallas_call(
        _ffn_in_kernel,
        out_shape=(jax.ShapeDtypeStruct((n_tok, D_MODEL), F32), jax.ShapeDtypeStruct((n_tok, IN_WIDTH), BF16)),
        grid=(n_tok // tm,),
        in_specs=[pl.BlockSpec((tm, D_MODEL), row), _layer_spec((1, D_MODEL), layer),
                  _layer_spec((D_MODEL, 2 * D_FF), layer), _layer_spec((D_FF, D_MODEL), layer),
                  _layer_spec((1, D_MODEL), layer), _layer_spec((D_MODEL, IN_WIDTH), layer)],
        out_specs=(pl.BlockSpec((tm, D_MODEL), row), pl.BlockSpec((tm, IN_WIDTH), row)),
        scratch_shapes=[pltpu.VMEM((tm, D_MODEL), BF16), pltpu.VMEM((tm, D_FF), BF16)],
        compiler_params=_params("parallel"),
        name="ffn_in",
    )(h, g1, wgu, wd, gm, win)


def _out_ffn(layer, o_sb, o_sw, h, gsb, gsw, wout, g2, wgu, wd, gf, final_norm):
    n_tok = h.shape[0]
    tm = OUT_FFN_ROW_GROUPS * TOKEN_TILE
    row = lambda i: (i, 0)
    return pl.pallas_call(
        functools.partial(_out_ffn_kernel, final_norm),
        out_shape=jax.ShapeDtypeStruct((n_tok, D_MODEL), F32),
        grid=(n_tok // tm,),
        in_specs=[pl.BlockSpec((tm, SB_WIDTH), row), pl.BlockSpec((tm, SWA_WIDTH), row),
                  pl.BlockSpec((tm, D_MODEL), row), _layer_spec((1, SB_WIDTH), layer),
                  _layer_spec((1, SWA_WIDTH), layer), _layer_spec((MIX_WIDTH, D_MODEL), layer),
                  _layer_spec((1, D_MODEL), layer), _layer_spec((D_MODEL, 2 * D_FF), layer),
                  _layer_spec((D_FF, D_MODEL), layer), _const_spec((1, D_MODEL))],
        out_specs=pl.BlockSpec((tm, D_MODEL), row),
        scratch_shapes=[pltpu.VMEM((tm, D_MODEL), BF16), pltpu.VMEM((tm, D_FF), BF16)],
        compiler_params=_params("parallel"),
        name="out_ffn",
    )(o_sb, o_sw, h, gsb, gsw, wout, g2, wgu, wd, gf)


def _sb_attention(proj, u2, batch, seq):
    T = SB_TILE
    pairs = SB_WIDTH // LANES
    diag_slots, off_slots = _sb_schedule(seq // T)
    table = jnp.asarray(np.array(diag_slots + off_slots, dtype=np.int32).reshape(-1))
    init_tiles = tuple(s[2] for s in diag_slots if s[4])
    col_block = lambda c: (lambda b, p, tbl: (b, c * pairs + p))
    return pl.pallas_call(
        functools.partial(_sb_kernel, len(diag_slots), len(off_slots), init_tiles),
        out_shape=jax.ShapeDtypeStruct((batch * seq, SB_WIDTH), F32),
        grid_spec=pltpu.PrefetchScalarGridSpec(
            num_scalar_prefetch=1,
            grid=(batch, pairs),
            in_specs=[pl.BlockSpec((seq, LANES), col_block(0)), pl.BlockSpec((seq, LANES), col_block(1)),
                      pl.BlockSpec((seq, LANES), col_block(2)),
                      pl.BlockSpec((T, T), lambda b, p, tbl: (0, 0), pipeline_mode=pl.Buffered(1))],
            out_specs=pl.BlockSpec((seq, LANES), col_block(0)),
            scratch_shapes=[pltpu.VMEM((2, seq, LANES), F32), pltpu.VMEM((2, seq, LANES), F32)]),
        compiler_params=_params("parallel", "parallel"),
        name="sb_attn",
    )(table, proj, proj, proj, u2)


def _swa_attention(layer, proj, sinks, bias, batch, seq):
    q_col = 3 * SB_WIDTH // SWA_WIDTH
    k_col = (3 * SB_WIDTH + SWA_WIDTH) // KV_WIDTH
    col_block = lambda col: (lambda b: (b, col))
    return pl.pallas_call(
        functools.partial(_swa_kernel, layer),
        out_shape=jax.ShapeDtypeStruct((batch * seq, SWA_WIDTH), F32),
        grid=(batch,),
        in_specs=[pl.BlockSpec(memory_space=pltpu.SMEM),
                  pl.BlockSpec((seq, SWA_WIDTH), col_block(q_col)),
                  pl.BlockSpec((seq, KV_WIDTH), col_block(k_col)), pl.BlockSpec((seq, KV_WIDTH), col_block(k_col + 1)),
                  _const_spec((2, SWA_Q_HEADS, BLOCK, BLOCK))],
        out_specs=pl.BlockSpec((seq, SWA_WIDTH), col_block(0)),
        compiler_params=_params("parallel"),
        name="swa_attn",
    )(sinks, proj, proj, proj, bias)


def _pair_swa_heads(x, axis):
    shape = x.shape
    k = shape[axis] // SWA_Q_HEADS
    x = x.reshape(shape[:axis] + (SWA_KV_HEADS, SWA_GROUP, k) + shape[axis + 1:])
    return jnp.swapaxes(x, axis, axis + 1).reshape(shape)


def _swa_folded_bias(rel_bias):
    table = _swa_bias_table(rel_bias)
    a = np.arange(BLOCK)[:, None]
    c = np.arange(BLOCK)[None, :]
    upper = jnp.asarray(c > a)
    normal = jnp.where(upper, table[:, :, :BLOCK], table[:, :, BLOCK:])
    first = jnp.where(upper, NEG_INF, table[:, :, BLOCK:])
    return _pair_swa_heads(jnp.stack([normal, first]), 1)


def kernel(x, norm_ffn1, w_ffn1_gu, w_ffn1_down, norm_mix, w_in, sinks, norm_out_sb, norm_out_swa, w_out,
           norm_ffn2, w_ffn2_gu, w_ffn2_down, rel_bias, norm_final):
    batch, seq, d_model = x.shape
    depth = w_in.shape[0]
    assert d_model == D_MODEL and seq % SB_TILE == 0 and (batch * seq) % (OUT_FFN_ROW_GROUPS * TOKEN_TILE) == 0
    assert WINDOW == BLOCK, "the sliding-window kernel folds the band assuming window == block"

    j = np.arange(SB_TILE)[:, None]
    s = np.arange(SB_TILE)[None, :]
    u2 = jnp.asarray(j >= s, dtype=BF16)
    bias = _swa_folded_bias(rel_bias)
    q_lo, q_hi = 3 * SB_WIDTH, 3 * SB_WIDTH + SWA_WIDTH

    gain = lambda g: g.astype(F32).reshape(depth, 1, -1)
    win = jnp.concatenate([w_in[:, :, :q_lo], _pair_swa_heads(w_in[:, :, q_lo:q_hi], 2), w_in[:, :, q_hi:]],
                          axis=2).astype(BF16)
    wout = jnp.concatenate([w_out[:, :SB_WIDTH], _pair_swa_heads(w_out[:, SB_WIDTH:], 1)], axis=1).astype(BF16)
    wgu1, wd1 = w_ffn1_gu.astype(BF16), w_ffn1_down.astype(BF16)
    wgu2, wd2 = w_ffn2_gu.astype(BF16), w_ffn2_down.astype(BF16)
    g1, gm, g2, gsb = gain(norm_ffn1), gain(norm_mix), gain(norm_ffn2), gain(norm_out_sb)
    gsw = gain(_pair_swa_heads(norm_out_swa, 1))
    gf = norm_final.astype(F32).reshape(1, -1)
    sinks_paired = _pair_swa_heads(sinks.astype(F32), 1)

    h = x.reshape(batch * seq, d_model)
    for l in range(depth):
        h, proj = _ffn_in(l, h, g1, wgu1, wd1, gm, win)
        o_sb = _sb_attention(proj, u2, batch, seq)
        o_sw = _swa_attention(l, proj, sinks_paired, bias, batch, seq)
        h = _out_ffn(l, o_sb, o_sw, h, gsb, gsw, wout, g2, wgu2, wd2, gf, final_norm=(l == depth - 1))
    return h.reshape(batch, seq, d_model)
```

```python
import functools
import math

import jax
import jax.numpy as jnp
import numpy as np
from jax import lax
from jax.experimental import pallas as pl
from jax.experimental.pallas import tpu as pltpu

D_MODEL = 1024
HEAD_DIM = 64
SB_HEADS = 8
SWA_Q_HEADS = 8
SWA_KV_HEADS = 2
SWA_GROUP = SWA_Q_HEADS // SWA_KV_HEADS
WINDOW = 128
BLOCK = 128
N_BUCKETS = 32
MAX_DISTANCE = 128
D_FF = 2816
EPS = 1e-6
NEG_INF = -1e30

SB_WIDTH = SB_HEADS * HEAD_DIM
SWA_WIDTH = SWA_Q_HEADS * HEAD_DIM
KV_WIDTH = SWA_KV_HEADS * HEAD_DIM
MIX_WIDTH = SB_WIDTH + SWA_WIDTH
IN_WIDTH = 3 * SB_WIDTH + SWA_WIDTH + 2 * KV_WIDTH
Q_SCALE = HEAD_DIM ** -0.5
LOG2E = math.log2(math.e)

LANES = 128
V7X_VMEM_BYTES = 64 * 1024 * 1024
VMEM_LIMIT_BYTES = V7X_VMEM_BYTES * 7 // 8
BF16_SIGN_BIT = 0x8000

TOKEN_TILE = 256
OUT_FFN_ROW_GROUPS = 4
FF_CHUNK = 256
PROJ_CHUNK = 256
SB_TILE = 256
SLOT_FIELDS = 5
SLOT_GROUP = 9
SWA_BLOCK_GROUP = 16

F32 = jnp.float32
BF16 = jnp.bfloat16


def _rms_norm(x, g):
    return x * lax.rsqrt(jnp.mean(x * x, axis=-1, keepdims=True) + EPS) * g


def _const_spec(shape):
    return pl.BlockSpec(shape, lambda *_: (0,) * len(shape), pipeline_mode=pl.Buffered(1))


def _layer_spec(shape, layer):
    return pl.BlockSpec((None,) + shape, lambda *_: (layer,) + (0,) * len(shape), pipeline_mode=pl.Buffered(1))


def _norm_rows(n_scr, rows, x, g_ref):
    n_scr[rows, :] = _rms_norm(x, g_ref[...]).astype(BF16)


def _gate_up_rows(rows, wgu_ref, n_scr, a_scr):
    for c in range(D_FF // FF_CHUNK):
        lo = c * FF_CHUNK
        n = n_scr[rows, :]
        gate = jnp.dot(n, wgu_ref[:, lo:lo + FF_CHUNK], preferred_element_type=F32)
        up = jnp.dot(n, wgu_ref[:, D_FF + lo:D_FF + lo + FF_CHUNK], preferred_element_type=F32)
        act = gate / (1.0 + jnp.exp(-gate)) * up
        a_scr[rows, lo:lo + FF_CHUNK] = act.astype(BF16)


def _down_rows(rows, h, wd_ref, a_scr):
    return h + 0.5 * jnp.dot(a_scr[rows, :], wd_ref[...], preferred_element_type=F32)


def _project_rows(rows, win_ref, n_scr, proj_ref):
    for c in range(IN_WIDTH // PROJ_CHUNK):
        lo = c * PROJ_CHUNK
        p = jnp.dot(n_scr[rows, :], win_ref[:, lo:lo + PROJ_CHUNK], preferred_element_type=F32)
        if lo < SB_WIDTH:
            p = p * (Q_SCALE * LOG2E)
        elif 3 * SB_WIDTH <= lo < 3 * SB_WIDTH + SWA_WIDTH:
            p = p * Q_SCALE
        proj_ref[rows, lo:lo + PROJ_CHUNK] = p.astype(BF16)


HALF_A = slice(0, TOKEN_TILE)
HALF_B = slice(TOKEN_TILE, 2 * TOKEN_TILE)


def _ffn_in_kernel(h_ref, g1_ref, wgu_ref, wd_ref, gm_ref, win_ref, h_out_ref, proj_ref, n_scr, a_scr):
    _norm_rows(n_scr, HALF_A, h_ref[HALF_A, :], g1_ref)
    _gate_up_rows(HALF_A, wgu_ref, n_scr, a_scr)
    _norm_rows(n_scr, HALF_B, h_ref[HALF_B, :], g1_ref)
    h1a = _down_rows(HALF_A, h_ref[HALF_A, :], wd_ref, a_scr)
    h_out_ref[HALF_A, :] = h1a
    _gate_up_rows(HALF_B, wgu_ref, n_scr, a_scr)
    _norm_rows(n_scr, HALF_A, h_out_ref[HALF_A, :], gm_ref)
    h1b = _down_rows(HALF_B, h_ref[HALF_B, :], wd_ref, a_scr)
    h_out_ref[HALF_B, :] = h1b
    _project_rows(HALF_A, win_ref, n_scr, proj_ref)
    _norm_rows(n_scr, HALF_B, h_out_ref[HALF_B, :], gm_ref)
    _project_rows(HALF_B, win_ref, n_scr, proj_ref)


def _out_ffn_kernel(final_norm, osb_ref, osw_ref, h_ref, gsb_ref, gsw_ref, wout_ref, g2_ref, wgu_ref, wd_ref,
                    gf_ref, out_ref, n_scr, a_scr):
    def mix(rows):
        sb = _rms_norm(osb_ref[rows, :], gsb_ref[...]).astype(BF16)
        sw = _rms_norm(osw_ref[rows, :], gsw_ref[...]).astype(BF16)
        out_ref[rows, :] = (h_ref[rows, :]
                            + jnp.dot(sb, wout_ref[:SB_WIDTH, :], preferred_element_type=F32)
                            + jnp.dot(sw, wout_ref[SB_WIDTH:, :], preferred_element_type=F32))
        _norm_rows(n_scr, rows, out_ref[rows, :], g2_ref)

    def finish(rows):
        h3 = _down_rows(rows, out_ref[rows, :], wd_ref, a_scr)
        out_ref[rows, :] = _rms_norm(h3, gf_ref[...]) if final_norm else h3

    groups = [slice(r, r + TOKEN_TILE) for r in range(0, out_ref.shape[0], TOKEN_TILE)]
    for prev, cur in zip([None] + groups, groups + [None]):
        if cur is not None:
            mix(cur)
        if prev is not None:
            finish(prev)
        if cur is not None:
            _gate_up_rows(cur, wgu_ref, n_scr, a_scr)


def _sb_schedule(nq):
    nxt = list(range(nq))
    remaining = lambda a: nxt[a] + 1
    diag_slots, off_slots = [], []
    diag_left = list(range(nq - 1, -1, -1))
    while diag_left:
        a = diag_left.pop(0)
        nxt[a] -= 1
        ready = [b for b in range(nq) if b != a and 0 <= nxt[b] < b]
        if ready:
            b = max(ready, key=remaining)
            diag_slots.append((a, a, b, nxt[b], 0))
        else:
            b = diag_left.pop(0)
            diag_slots.append((a, a, b, b, 1))
        nxt[b] -= 1
    while any(n >= 0 for n in nxt):
        b1, b2 = sorted((b for b in range(nq) if nxt[b] >= 0), key=remaining, reverse=True)[:2]
        off_slots.append((b1, nxt[b1], b2, nxt[b2], 0))
        nxt[b1] -= 1
        nxt[b2] -= 1
    return diag_slots, off_slots


def _sb_kernel(n_diag_slots, n_off_slots, init_tiles, tbl_ref, q_ref, k_ref, v_ref, u_ref, o_ref, acc_ref, r_ref):
    T = SB_TILE
    H = T // 2
    iota = lambda shape, axis: lax.broadcasted_iota(jnp.int32, shape, axis)
    causal = iota((T, T), 1) < iota((T, T), 0)
    causal_upper = iota((H, H), 1) < iota((H, H), 0)
    causal_lower = iota((H, T), 1) < iota((H, T), 0) + H
    head_lanes = {n: [iota((n, LANES), 1) < HEAD_DIM, iota((n, LANES), 1) >= HEAD_DIM] for n in (H, T)}

    for qt in init_tiles:
        acc_ref[:, qt * T:(qt + 1) * T, :] = jnp.zeros((2, T, LANES), F32)
        r_ref[:, qt * T:(qt + 1) * T, :] = jnp.zeros((2, T, LANES), F32)

    class Slot:
        def __init__(self, index, masked):
            base = index * SLOT_FIELDS
            self.items = []
            for n in range(2):
                qs = pl.multiple_of(tbl_ref[base + 2 * n] * T, T)
                ks = pl.multiple_of(tbl_ref[base + 2 * n + 1] * T, T)
                for hh in range(2):
                    if masked and n == 0:
                        self.items.append((hh, pl.ds(qs, H), pl.ds(ks, H), causal_upper, True))
                        self.items.append((hh, pl.ds(pl.multiple_of(qs + H, H), H), pl.ds(ks, T), causal_lower, True))
                    else:
                        mask = causal | (tbl_ref[base + 4] == 0) if masked else None
                        self.items.append((hh, pl.ds(qs, T), pl.ds(ks, T), mask, False))

        def scores(self):
            self.z = []
            for hh, q_rows, k_rows, mask, _ in self.items:
                q = q_ref[q_rows, :]
                qh = jnp.where(head_lanes[q.shape[0]][hh], q, jnp.zeros_like(q))
                z = lax.dot_general(qh, k_ref[k_rows, :], (((1,), (1,)), ((), ())), preferred_element_type=F32)
                if mask is not None:
                    z = jnp.where(mask, z, NEG_INF)
                self.z.append(z)

        def softplus(self):
            self.lhs = []
            for z in self.z:
                zb = z.astype(BF16)
                neg_abs = lax.bitcast_convert_type(
                    lax.bitcast_convert_type(zb, jnp.uint16) | jnp.uint16(BF16_SIGN_BIT), BF16)
                log2_1p = jnp.log(1.0 + jnp.exp2(neg_abs)) * jnp.asarray(LOG2E, BF16)
                self.lhs.append(jnp.maximum(zb, 0.0) + log2_1p)

        def suffix_sums(self):
            self.incl = [jnp.dot(x, u_ref[:x.shape[1], :x.shape[1]], preferred_element_type=F32) for x in self.lhs]

        def weights(self):
            self.w = []
            for (hh, q_rows, _, _, first), z, incl in zip(self.items, self.z, self.incl):
                row_sum = jnp.broadcast_to(incl[:, 0:1], (z.shape[0], LANES))
                if first:
                    self.w.append(jnp.exp2(z - incl).astype(BF16))
                    r_ref[hh, q_rows, :] = row_sum
                else:
                    r = r_ref[hh, q_rows, :]
                    self.w.append(jnp.exp2(z - incl - jnp.concatenate([r] * (z.shape[1] // LANES), axis=1)).astype(BF16))
                    r_ref[hh, q_rows, :] = r + row_sum

        def values(self):
            self.pv = [jnp.dot(w, v_ref[k_rows, :], preferred_element_type=F32)
                       for (_, _, k_rows, _, _), w in zip(self.items, self.w)]

        def accumulate(self):
            for (hh, q_rows, _, _, first), pv in zip(self.items, self.pv):
                if first:
                    acc_ref[hh, q_rows, :] = pv
                else:
                    acc_ref[hh, q_rows, :] += pv

    def staggered(slots):
        for prev, cur in zip([None] + slots, slots + [None]):
            if cur is not None:
                cur.scores()
            if prev is not None:
                prev.weights()
                prev.values()
            if cur is not None:
                cur.softplus()
            if prev is not None:
                prev.accumulate()
            if cur is not None:
                cur.suffix_sums()

    def group_loop(first, n_groups, masked):
        def body(i, _):
            staggered([Slot(first + SLOT_GROUP * i + n, masked) for n in range(SLOT_GROUP)])
            return 0
        lax.fori_loop(0, n_groups, body, 0)

    n_diag_groups = n_diag_slots // SLOT_GROUP
    n_mixed_diag = n_diag_slots % SLOT_GROUP
    n_mixed_off = (SLOT_GROUP - n_mixed_diag) % SLOT_GROUP
    assert (n_off_slots - n_mixed_off) % SLOT_GROUP == 0
    group_loop(0, n_diag_groups, True)
    if n_mixed_diag:
        staggered([Slot(n_diag_groups * SLOT_GROUP + n, True) for n in range(n_mixed_diag)]
                  + [Slot(n_diag_slots + n, False) for n in range(n_mixed_off)])
    group_loop(n_diag_slots + n_mixed_off, (n_off_slots - n_mixed_off) // SLOT_GROUP, False)
    o_ref[...] = jnp.where(lax.broadcasted_iota(jnp.int32, o_ref.shape, 1) < HEAD_DIM, acc_ref[0], acc_ref[1])


def _swa_kernel(layer, sink_ref, q_ref, k_ref, v_ref, bias_ref, o_ref):
    n_blocks = q_ref.shape[0] // BLOCK
    n_slots = SWA_Q_HEADS // 2
    lane = lax.broadcasted_iota(jnp.int32, (BLOCK, LANES), 1)
    a = lax.broadcasted_iota(jnp.int32, (BLOCK, BLOCK), 0)
    c = lax.broadcasted_iota(jnp.int32, (BLOCK, BLOCK), 1)
    upper = c > a
    half = [lane < HEAD_DIM, lane >= HEAD_DIM]

    heads = [(p, sub) for p in range(n_slots) for sub in range(2)]

    class Block:
        def __init__(self, i):
            self.cur = pl.multiple_of(i * BLOCK, BLOCK)
            self.prev = pl.multiple_of(jnp.maximum(i - 1, 0) * BLOCK, BLOCK)
            self.table = 1 - jnp.minimum(i, 1)

        def rows(self, ref):
            return jnp.concatenate([ref[pl.ds(self.prev, BLOCK), :], ref[pl.ds(self.cur, BLOCK), :]], axis=0)

        def scores(self):
            kcat = self.rows(k_ref)
            self.s = []
            for p, sub in heads:
                q = q_ref[pl.ds(self.cur, BLOCK), p * LANES:(p + 1) * LANES]
                s = lax.dot_general(jnp.where(half[sub], q, jnp.zeros_like(q)), kcat, (((1,), (1,)), ((), ())),
                                    preferred_element_type=F32)
                self.s.append(jnp.where(upper, s[:, :BLOCK], s[:, BLOCK:]) + bias_ref[self.table, 2 * p + sub])

        def softmax(self):
            self.p = []
            for (p, sub), s in zip(heads, self.s):
                sink = sink_ref[layer, 2 * p + sub]
                m = jnp.maximum(jnp.max(s, axis=-1, keepdims=True), sink)
                e = jnp.exp(s - m)
                e = (e * (1.0 / (jnp.sum(e, axis=-1, keepdims=True) + jnp.exp(sink - m)))).astype(BF16)
                zero = jnp.zeros_like(e)
                self.p.append(jnp.concatenate([jnp.where(upper, e, zero), jnp.where(upper, zero, e)], axis=1))

        def values(self):
            vcat = self.rows(v_ref)
            self.o = [jnp.dot(pr, vcat, preferred_element_type=F32) for pr in self.p]

        def store(self):
            for p in range(n_slots):
                o_ref[pl.ds(self.cur, BLOCK), p * LANES:(p + 1) * LANES] = jnp.where(
                    half[0], self.o[2 * p], self.o[2 * p + 1])

    def block_group(i, _):
        blocks = [Block(SWA_BLOCK_GROUP * i + n) for n in range(SWA_BLOCK_GROUP)]
        for prev, cur in zip([None] + blocks, blocks + [None]):
            if cur is not None:
                cur.scores()
            if prev is not None:
                prev.values()
            if cur is not None:
                cur.softmax()
            if prev is not None:
                prev.store()
        return 0

    assert n_blocks % SWA_BLOCK_GROUP == 0
    lax.fori_loop(0, n_blocks // SWA_BLOCK_GROUP, block_group, 0)


def _t5_causal_bucket(dist):
    max_exact = N_BUCKETS // 2
    d = jnp.maximum(dist, 1).astype(F32)
    large = max_exact + (jnp.log(d / max_exact) / math.log(MAX_DISTANCE / max_exact)
                         * (N_BUCKETS - max_exact)).astype(jnp.int32)
    large = jnp.minimum(large, N_BUCKETS - 1)
    return jnp.where(dist < max_exact, dist, large)


def _swa_bias_table(rel_bias):
    period = 3 * BLOCK - 1
    dist = jnp.arange(period) - (BLOCK - 1)
    g = rel_bias.astype(F32)[_t5_causal_bucket(jnp.maximum(dist, 0))].T
    flat = jnp.tile(g, (1, BLOCK + 1))[:, :BLOCK * (period + 1)]
    return flat.reshape(SWA_Q_HEADS, BLOCK, period + 1)[:, :, :2 * BLOCK][:, :, ::-1]


def _params(*semantics):
    return pltpu.CompilerParams(dimension_semantics=semantics, vmem_limit_bytes=VMEM_LIMIT_BYTES)


def _ffn_in(layer, h, g1, wgu, wd, gm, win):
    n_tok = h.shape[0]
    tm = 2 * TOKEN_TILE
    row = lambda i: (i, 0)
    return pl.pallas_call(
        _ffn_in_kernel,
        out_shape=(jax.ShapeDtypeStruct((n_tok, D_MODEL), F32), jax.ShapeDtypeStruct((n_tok, IN_WIDTH), BF16)),
        grid=(n_tok // tm,),
        in_specs=[pl.BlockSpec((tm, D_MODEL), row), _layer_spec((1, D_MODEL), layer),
                  _layer_spec((D_MODEL, 2 * D_FF), layer), _layer_spec((D_FF, D_MODEL), layer),
                  _layer_spec((1, D_MODEL), layer), _layer_spec((D_MODEL, IN_WIDTH), layer)],
        out_specs=(pl.BlockSpec((tm, D_MODEL), row), pl.BlockSpec((tm, IN_WIDTH), row)),
        scratch_shapes=[pltpu.VMEM((tm, D_MODEL), BF16), pltpu.VMEM((tm, D_FF), BF16)],
        compiler_params=_params("parallel"),
        name="ffn_in",
    )(h, g1, wgu, wd, gm, win)


def _out_ffn(layer, o_sb, o_sw, h, gsb, gsw, wout, g2, wgu, wd, gf, final_norm):
    n_tok = h.shape[0]
    tm = OUT_FFN_ROW_GROUPS * TOKEN_TILE
    row = lambda i: (i, 0)
    return pl.pallas_call(
        functools.partial(_out_ffn_kernel, final_norm),
        out_shape=jax.ShapeDtypeStruct((n_tok, D_MODEL), F32),
        grid=(n_tok // tm,),
        in_specs=[pl.BlockSpec((tm, SB_WIDTH), row), pl.BlockSpec((tm, SWA_WIDTH), row),
                  pl.BlockSpec((tm, D_MODEL), row), _layer_spec((1, SB_WIDTH), layer),
                  _layer_spec((1, SWA_WIDTH), layer), _layer_spec((MIX_WIDTH, D_MODEL), layer),
                  _layer_spec((1, D_MODEL), layer), _layer_spec((D_MODEL, 2 * D_FF), layer),
                  _layer_spec((D_FF, D_MODEL), layer), _const_spec((1, D_MODEL))],
        out_specs=pl.BlockSpec((tm, D_MODEL), row),
        scratch_shapes=[pltpu.VMEM((tm, D_MODEL), BF16), pltpu.VMEM((tm, D_FF), BF16)],
        compiler_params=_params("parallel"),
        name="out_ffn",
    )(o_sb, o_sw, h, gsb, gsw, wout, g2, wgu, wd, gf)


def _sb_attention(proj, u2, batch, seq):
    T = SB_TILE
    pairs = SB_WIDTH // LANES
    diag_slots, off_slots = _sb_schedule(seq // T)
    table = jnp.asarray(np.array(diag_slots + off_slots, dtype=np.int32).reshape(-1))
    init_tiles = tuple(s[2] for s in diag_slots if s[4])
    col_block = lambda c: (lambda b, p, tbl: (b, c * pairs + p))
    return pl.pallas_call(
        functools.partial(_sb_kernel, len(diag_slots), len(off_slots), init_tiles),
        out_shape=jax.ShapeDtypeStruct((batch * seq, SB_WIDTH), F32),
        grid_spec=pltpu.PrefetchScalarGridSpec(
            num_scalar_prefetch=1,
            grid=(batch, pairs),
            in_specs=[pl.BlockSpec((seq, LANES), col_block(0)), pl.BlockSpec((seq, LANES), col_block(1)),
                      pl.BlockSpec((seq, LANES), col_block(2)),
                      pl.BlockSpec((T, T), lambda b, p, tbl: (0, 0), pipeline_mode=pl.Buffered(1))],
            out_specs=pl.BlockSpec((seq, LANES), col_block(0)),
            scratch_shapes=[pltpu.VMEM((2, seq, LANES), F32), pltpu.VMEM((2, seq, LANES), F32)]),
        compiler_params=_params("parallel", "parallel"),
        name="sb_attn",
    )(table, proj, proj, proj, u2)


def _swa_attention(layer, proj, sinks, bias, batch, seq):
    q_col = 3 * SB_WIDTH // SWA_WIDTH
    k_col = (3 * SB_WIDTH + SWA_WIDTH) // KV_WIDTH
    col_block = lambda col: (lambda b: (b, col))
    return pl.pallas_call(
        functools.partial(_swa_kernel, layer),
        out_shape=jax.ShapeDtypeStruct((batch * seq, SWA_WIDTH), F32),
        grid=(batch,),
        in_specs=[pl.BlockSpec(memory_space=pltpu.SMEM),
                  pl.BlockSpec((seq, SWA_WIDTH), col_block(q_col)),
                  pl.BlockSpec((seq, KV_WIDTH), col_block(k_col)), pl.BlockSpec((seq, KV_WIDTH), col_block(k_col + 1)),
                  _const_spec((2, SWA_Q_HEADS, BLOCK, BLOCK))],
        out_specs=pl.BlockSpec((seq, SWA_WIDTH), col_block(0)),
        compiler_params=_params("parallel"),
        name="swa_attn",
    )(sinks, proj, proj, proj, bias)


def _pair_swa_heads(x, axis):
    shape = x.shape
    k = shape[axis] // SWA_Q_HEADS
    x = x.reshape(shape[:axis] + (SWA_KV_HEADS, SWA_GROUP, k) + shape[axis + 1:])
    return jnp.swapaxes(x, axis, axis + 1).reshape(shape)


def _swa_folded_bias(rel_bias):
    table = _swa_bias_table(rel_bias)
    a = np.arange(BLOCK)[:, None]
    c = np.arange(BLOCK)[None, :]
    upper = jnp.asarray(c > a)
    normal = jnp.where(upper, table[:, :, :BLOCK], table[:, :, BLOCK:])
    first = jnp.where(upper, NEG_INF, table[:, :, BLOCK:])
    return _pair_swa_heads(jnp.stack([normal, first]), 1)


def kernel(x, norm_ffn1, w_ffn1_gu, w_ffn1_down, norm_mix, w_in, sinks, norm_out_sb, norm_out_swa, w_out,
           norm_ffn2, w_ffn2_gu, w_ffn2_down, rel_bias, norm_final):
    batch, seq, d_model = x.shape
    depth = w_in.shape[0]
    assert d_model == D_MODEL and seq % SB_TILE == 0 and (batch * seq) % (OUT_FFN_ROW_GROUPS * TOKEN_TILE) == 0
    assert WINDOW == BLOCK, "the sliding-window kernel folds the band assuming window == block"

    j = np.arange(SB_TILE)[:, None]
    s = np.arange(SB_TILE)[None, :]
    u2 = jnp.asarray(j >= s, dtype=BF16)
    bias = _swa_folded_bias(rel_bias)
    q_lo, q_hi = 3 * SB_WIDTH, 3 * SB_WIDTH + SWA_WIDTH

    gain = lambda g: g.astype(F32).reshape(depth, 1, -1)
    win = jnp.concatenate([w_in[:, :, :q_lo], _pair_swa_heads(w_in[:, :, q_lo:q_hi], 2), w_in[:, :, q_hi:]],
                          axis=2).astype(BF16)
    wout = jnp.concatenate([w_out[:, :SB_WIDTH], _pair_swa_heads(w_out[:, SB_WIDTH:], 1)], axis=1).astype(BF16)
    wgu1, wd1 = w_ffn1_gu.astype(BF16), w_ffn1_down.astype(BF16)
    wgu2, wd2 = w_ffn2_gu.astype(BF16), w_ffn2_down.astype(BF16)
    g1, gm, g2, gsb = gain(norm_ffn1), gain(norm_mix), gain(norm_ffn2), gain(norm_out_sb)
    gsw = gain(_pair_swa_heads(norm_out_swa, 1))
    gf = norm_final.astype(F32).reshape(1, -1)
    sinks_paired = _pair_swa_heads(sinks.astype(F32), 1)

    h = x.reshape(batch * seq, d_model)
    for l in range(depth):
        h, proj = _ffn_in(l, h, g1, wgu1, wd1, gm, win)
        o_sb = _sb_attention(proj, u2, batch, seq)
        o_sw = _swa_attention(l, proj, sinks_paired, bias, batch, seq)
        h = _out_ffn(l, o_sb, o_sw, h, gsb, gsw, wout, g2, wgu2, wd2, gf, final_norm=(l == depth - 1))
    return h.reshape(batch, seq, d_model)
```
